```python
import jax, jax.numpy as jnp
from jax import lax
import numpy as np

D_MODEL = 2048
BATCH = 32
SEQ = 256
DEPTH = 4
DEC_BATCH = 4
DEC_SEQ = 4096
PAST_LEN = 256

GRID_W = 64
N_MIXERS = 3
N_ATTN_LAYERS = (DEPTH + 2) // 3
N_CONV_LAYERS = (DEPTH + 1) // 3
N_RWKV_LAYERS = DEPTH // 3
HEAD_DIM = 128
N_HEADS = D_MODEL // HEAD_DIM
N_KV_HEADS = N_HEADS // 4
Q_BLOCK = 128
ROPE_THETA = 10000.0
CONV_WIDTH = 31
RWKV_HEAD = 64
RWKV_HEADS = D_MODEL // RWKV_HEAD
DECAY_LORA = 96
ICLR_LORA = 96
GATE_LORA = 256
GN_EPS = 64e-5
N_KEYS = 128
N_EXPERTS = N_KEYS * N_KEYS
PEER_HEADS = 8
PEER_TOPK = 16
PEER_KEY_DIM = 256
TOKEN_BLOCK = 128
NORM_EPS = 1e-6
LN_EPS = 1e-5

kernel_name = 'hybrid_diffusion_prefix_step'


def rms_norm(x, g):
    xf = x.astype(jnp.float32)
    y = xf * lax.rsqrt(jnp.mean(xf * xf, axis=-1, keepdims=True) + NORM_EPS)
    return (y * g.astype(jnp.float32)).astype(x.dtype)


def layer_norm(x, g, b):
    xf = x.astype(jnp.float32)
    xc = xf - jnp.mean(xf, axis=-1, keepdims=True)
    var = jnp.mean(xc * xc, axis=-1, keepdims=True)
    return (xc * lax.rsqrt(var + LN_EPS) * g.astype(jnp.float32) + b.astype(jnp.float32)).astype(x.dtype)


def ada_mods(cond, w, b):
    m = (jax.nn.silu(cond) @ w + b)[..., None, :]
    return jnp.split(m, 6, axis=-1)


def modulate(x, g, shift, scale):
    return rms_norm(x, g) * (1 + scale) + shift


def axial_rope_tables(rows):
    pos_row = jnp.repeat(jnp.arange(rows, dtype=jnp.float32), GRID_W)
    pos_col = jnp.tile(jnp.arange(GRID_W, dtype=jnp.float32), rows)
    n_freq = HEAD_DIM // 4
    inv = ROPE_THETA ** (-jnp.arange(n_freq, dtype=jnp.float32) / n_freq)
    ang = jnp.stack([pos_row[:, None] * inv, pos_col[:, None] * inv], axis=1)
    return jnp.cos(ang), jnp.sin(ang)


def apply_axial_rope(x, cos, sin):
    b, t, h, _ = x.shape
    xr = x.astype(jnp.float32).reshape(b, t, h, 2, 2, HEAD_DIM // 4)
    c = cos[None, :, None]
    s = sin[None, :, None]
    x1, x2 = xr[..., 0, :], xr[..., 1, :]
    out = jnp.stack([x1 * c - x2 * s, x2 * c + x1 * s], axis=-2)
    return out.reshape(b, t, h, HEAD_DIM).astype(x.dtype)


def gqa_blocked(q, k, v):
    b, tq, _, _ = q.shape
    g = N_HEADS // N_KV_HEADS
    nb = tq // Q_BLOCK
    qb = jnp.moveaxis(q.reshape(b, nb, Q_BLOCK, N_KV_HEADS, g, HEAD_DIM), 1, 0)
    scale = HEAD_DIM ** -0.5

    def one_block(q_blk):
        s = jnp.einsum('bqhgd,bkhd->bhgqk', q_blk, k, preferred_element_type=jnp.float32) * scale
        p = jax.nn.softmax(s, axis=-1).astype(v.dtype)
        return jnp.einsum('bhgqk,bkhd->bqhgd', p, v)

    o = lax.map(one_block, qb)
    return jnp.moveaxis(o, 0, 1).reshape(b, tq, N_HEADS * HEAD_DIM)


def attn_qkv(h, wqkv, q_norm, k_norm):
    b, t, _ = h.shape
    qkv = h @ wqkv
    nq = N_HEADS * HEAD_DIM
    nk = N_KV_HEADS * HEAD_DIM
    q = qkv[..., :nq].reshape(b, t, N_HEADS, HEAD_DIM)
    k = qkv[..., nq:nq + nk].reshape(b, t, N_KV_HEADS, HEAD_DIM)
    v = qkv[..., nq + nk:].reshape(b, t, N_KV_HEADS, HEAD_DIM)
    return rms_norm(q, q_norm), rms_norm(k, k_norm), v


def conformer_conv(h, w1, b1, dw, dw_b, ln_g, ln_b, w2, b2):
    a, gt = jnp.split(h @ w1 + b1, 2, axis=-1)
    u = a * jax.nn.sigmoid(gt)
    pad = CONV_WIDTH // 2
    u = lax.conv_general_dilated(u, dw[:, None, :].astype(u.dtype), window_strides=(1,),
                                 padding=[(pad, pad)], dimension_numbers=('NWC', 'WIO', 'NWC'),
                                 feature_group_count=D_MODEL) + dw_b
    u = jax.nn.silu(layer_norm(u, ln_g, ln_b))
    return u @ w2 + b2


def wkv_scan(s0, r, w, k, v, a, b, reverse):
    def step(S, inp):
        r_t, w_t, k_t, v_t, a_t, b_t = inp
        sa = jnp.einsum('bhvk,bhk->bhv', S, a_t)
        S = S * w_t[:, :, None, :] + sa[..., None] * b_t[:, :, None, :] + v_t[..., None] * k_t[:, :, None, :]
        return S, jnp.einsum('bhvk,bhk->bhv', S, r_t)

    xs = tuple(jnp.moveaxis(z.astype(jnp.float32), 1, 0) for z in (r, w, k, v, a, b))
    s_final, ys = lax.scan(step, s0.astype(jnp.float32), xs, reverse=reverse)
    return s_final, jnp.moveaxis(ys, 0, 1)


def rwkv7_bidir(h, s0, mu, wr, wk, wv, wo, w0, w1, w2, a0, a1, a2, g1, g2, k_k, k_a, r_k, lnx_g, lnx_b):
    b, t, d = h.shape
    heads = lambda z: z.reshape(b, t, RWKV_HEADS, RWKV_HEAD)
    zero = jnp.zeros_like(h[:, :1])
    xx = 0.5 * (jnp.concatenate([zero, h[:, :-1]], 1) + jnp.concatenate([h[:, 1:], zero], 1)) - h
    xr, xw, xk, xv, xa, xg = [h + xx * mu[n] for n in range(6)]
    r = heads(xr @ wr).astype(jnp.float32)
    k = (xk @ wk).astype(jnp.float32)
    v = heads(xv @ wv).astype(jnp.float32)
    gate = jax.nn.sigmoid(xg @ g1) @ g2
    kk = heads(k * k_k)
    kk = kk / jnp.maximum(jnp.sqrt(jnp.sum(kk * kk, axis=-1, keepdims=True)), 1e-12)
    ys, bonuses, finals = [], [], []
    for dirn, rev in enumerate((False, True)):
        logw = -jax.nn.softplus(-(w0[dirn] + jnp.tanh(xw @ w1[dirn]) @ w2[dirn]).astype(jnp.float32)) - 0.5
        decay = heads(jnp.exp(-jnp.exp(logw)))
        iclr = jax.nn.sigmoid(a0[dirn] + (xa @ a1[dirn]) @ a2[dirn]).astype(jnp.float32)
        k_d = heads(k * (1 + (iclr - 1) * k_a))
        s_fin, y_d = wkv_scan(s0[:, dirn], r, decay, k_d, v, -kk, kk * heads(iclr), rev)
        ys.append(y_d)
        bonuses.append(jnp.sum(r * k_d * r_k, axis=-1, keepdims=True) * v)
        finals.append(s_fin)
    y = ys[0] + ys[1]
    yc = y - jnp.mean(y, axis=-1, keepdims=True)
    yn = yc * lax.rsqrt(jnp.mean(yc * yc, axis=-1, keepdims=True) + GN_EPS)
    yn = yn.reshape(b, t, d) * lnx_g + lnx_b
    out = ((yn + (bonuses[0] + bonuses[1]).reshape(b, t, d)).astype(h.dtype) * gate) @ wo
    return out, jnp.stack(finals, axis=1)


def peer_mixer(h, wq, sub_keys, u_tab, v_tab):
    b, t, d = h.shape
    x = h.reshape(b * t // TOKEN_BLOCK, TOKEN_BLOCK, d)
    kp = PEER_TOPK

    def one_block(xb):
        q = (xb @ wq).reshape(TOKEN_BLOCK, PEER_HEADS, 2, PEER_KEY_DIM // 2).astype(jnp.float32)
        s = jnp.einsum('nhpc,hpkc->nhpk', q, sub_keys.astype(jnp.float32))
        sv, si = lax.top_k(s, kp)
        cand = sv[:, :, 0, :, None] + sv[:, :, 1, None, :]
        cand_idx = si[:, :, 0, :, None] * N_KEYS + si[:, :, 1, None, :]
        cv, ci = lax.top_k(cand.reshape(TOKEN_BLOCK, PEER_HEADS, kp * kp), kp)
        idx = jnp.take_along_axis(cand_idx.reshape(TOKEN_BLOCK, PEER_HEADS, kp * kp), ci, axis=-1)
        g = jax.nn.softmax(cv, axis=-1).reshape(TOKEN_BLOCK, PEER_HEADS * kp)
        idx = idx.reshape(TOKEN_BLOCK, PEER_HEADS * kp)
        u = jnp.take(u_tab, idx, axis=0)
        act = jax.nn.gelu(jnp.einsum('nd,ned->ne', xb, u).astype(jnp.float32), approximate=False)
        wgt = (g * act).astype(xb.dtype)
        return jnp.einsum('ne,ned->nd', wgt, jnp.take(v_tab, idx, axis=0))

    return lax.map(one_block, x).reshape(b, t, d)


def setup_inputs(seed: int = 0) -> dict:
    key = jax.random.key(seed)
    keys = iter(jax.random.split(key, 64))

    def nrm(shape, scale=1.0):
        return jax.random.normal(next(keys), shape, jnp.float32) * scale

    def gain(shape):
        return 1.0 + nrm(shape, 0.05)

    def unif(shape, lo, hi):
        return jax.random.uniform(next(keys), shape, jnp.float32, lo, hi)

    D = D_MODEL
    fan = D ** -0.5
    qkv_out = (N_HEADS + 2 * N_KV_HEADS) * HEAD_DIM
    return {
        'x_prompt': nrm((BATCH, SEQ, D)),
        'x_sample': nrm((DEC_BATCH, DEC_SEQ, D)),
        'cache_k': nrm((DEC_BATCH, N_ATTN_LAYERS, PAST_LEN, N_KV_HEADS, HEAD_DIM)),
        'cache_v': nrm((DEC_BATCH, N_ATTN_LAYERS, PAST_LEN, N_KV_HEADS, HEAD_DIM)),
        'state_wkv': nrm((DEC_BATCH, N_RWKV_LAYERS, 2, RWKV_HEADS, RWKV_HEAD, RWKV_HEAD), 0.5),
        'c': nrm((DEC_BATCH, D)),
        'c_ctx': nrm((D,)),
        'norm1': gain((DEPTH, D)),
        'norm2': gain((DEPTH, D)),
        'ada_w': nrm((DEPTH, D, 6 * D), 0.5 * fan),
        'ada_b': nrm((DEPTH, 6 * D), 0.01),
        'attn_wqkv': nrm((N_ATTN_LAYERS, D, qkv_out), fan),
        'attn_q_norm': gain((N_ATTN_LAYERS, HEAD_DIM)),
        'attn_k_norm': gain((N_ATTN_LAYERS, HEAD_DIM)),
        'attn_wo': nrm((N_ATTN_LAYERS, N_HEADS * HEAD_DIM, D), (N_HEADS * HEAD_DIM) ** -0.5),
        'conv_w1': nrm((N_CONV_LAYERS, D, 2 * D), fan),
        'conv_b1': nrm((N_CONV_LAYERS, 2 * D), 0.01),
        'conv_dw': nrm((N_CONV_LAYERS, CONV_WIDTH, D), CONV_WIDTH ** -0.5),
        'conv_dw_b': nrm((N_CONV_LAYERS, D), 0.01),
        'conv_ln_g': gain((N_CONV_LAYERS, D)),
        'conv_ln_b': nrm((N_CONV_LAYERS, D), 0.01),
        'conv_w2': nrm((N_CONV_LAYERS, D, D), fan),
        'conv_b2': nrm((N_CONV_LAYERS, D), 0.01),
        'rwkv_mu': unif((N_RWKV_LAYERS, 6, D), 0.0, 1.0),
        'rwkv_wr': nrm((N_RWKV_LAYERS, D, D), fan),
        'rwkv_wk': nrm((N_RWKV_LAYERS, D, D), fan),
        'rwkv_wv': nrm((N_RWKV_LAYERS, D, D), fan),
        'rwkv_wo': nrm((N_RWKV_LAYERS, D, D), fan),
        'rwkv_w0': unif((N_RWKV_LAYERS, 2, D), -6.0, -1.0),
        'rwkv_w1': nrm((N_RWKV_LAYERS, 2, D, DECAY_LORA), 0.1 * fan),
        'rwkv_w2': nrm((N_RWKV_LAYERS, 2, DECAY_LORA, D), 0.1 * DECAY_LORA ** -0.5),
        'rwkv_a0': nrm((N_RWKV_LAYERS, 2, D), 0.1),
        'rwkv_a1': nrm((N_RWKV_LAYERS, 2, D, ICLR_LORA), 0.1 * fan),
        'rwkv_a2': nrm((N_RWKV_LAYERS, 2, ICLR_LORA, D), 0.1 * ICLR_LORA ** -0.5),
        'rwkv_g1': nrm((N_RWKV_LAYERS, D, GATE_LORA), fan),
        'rwkv_g2': nrm((N_RWKV_LAYERS, GATE_LORA, D), GATE_LORA ** -0.5),
        'rwkv_k_k': 0.85 + nrm((N_RWKV_LAYERS, D), 0.05),
        'rwkv_k_a': gain((N_RWKV_LAYERS, D)),
        'rwkv_r_k': nrm((N_RWKV_LAYERS, RWKV_HEADS, RWKV_HEAD), 0.1),
        'rwkv_lnx_g': gain((N_RWKV_LAYERS, D)),
        'rwkv_lnx_b': nrm((N_RWKV_LAYERS, D), 0.01),
        'peer_wq': nrm((DEPTH, D, PEER_HEADS * PEER_KEY_DIM), fan),
        'peer_keys': nrm((DEPTH, PEER_HEADS, 2, N_KEYS, PEER_KEY_DIM // 2), (PEER_KEY_DIM // 2) ** -0.5),
        'peer_u': nrm((DEPTH, N_EXPERTS, D), fan),
        'peer_v': nrm((DEPTH, N_EXPERTS, D), 0.5),
        'final_norm': gain((D,)),
    }


def reference(x_prompt, x_sample, cache_k, cache_v, state_wkv, c, c_ctx,
              norm1, norm2, ada_w, ada_b,
              attn_wqkv, attn_q_norm, attn_k_norm, attn_wo,
              conv_w1, conv_b1, conv_dw, conv_dw_b, conv_ln_g, conv_ln_b, conv_w2, conv_b2,
              rwkv_mu, rwkv_wr, rwkv_wk, rwkv_wv, rwkv_wo, rwkv_w0, rwkv_w1, rwkv_w2,
              rwkv_a0, rwkv_a1, rwkv_a2, rwkv_g1, rwkv_g2, rwkv_k_k, rwkv_k_a, rwkv_r_k,
              rwkv_lnx_g, rwkv_lnx_b,
              peer_wq, peer_keys, peer_u, peer_v, final_norm):
    rows = x_sample.shape[1] // GRID_W
    cos, sin = axial_rope_tables(rows)
    xc, xl = x_prompt, x_sample
    new_k, new_v, new_s = [], [], []
    for i in range(DEPTH):
        kind, j = i % N_MIXERS, i // N_MIXERS
        sh1c, sc1c, gt1c, sh2c, sc2c, gt2c = ada_mods(c_ctx, ada_w[i], ada_b[i])
        sh1l, sc1l, gt1l, sh2l, sc2l, gt2l = ada_mods(c, ada_w[i], ada_b[i])
        hc = modulate(xc, norm1[i], sh1c, sc1c)
        hl = modulate(xl, norm1[i], sh1l, sc1l)
        if kind == 0:
            qc, kc, vc = attn_qkv(hc, attn_wqkv[j], attn_q_norm[j], attn_k_norm[j])
            oc = gqa_blocked(qc, kc, vc) @ attn_wo[j]
            ql, kl, vl = attn_qkv(hl, attn_wqkv[j], attn_q_norm[j], attn_k_norm[j])
            ql = apply_axial_rope(ql, cos, sin)
            kl = apply_axial_rope(kl, cos, sin)
            keys = jnp.concatenate([kl, cache_k[:, j].astype(kl.dtype)], axis=1)
            vals = jnp.concatenate([vl, cache_v[:, j].astype(vl.dtype)], axis=1)
            ol = gqa_blocked(ql, keys, vals) @ attn_wo[j]
            new_k.append(kc)
            new_v.append(vc)
        elif kind == 1:
            conv_p = (conv_w1[j], conv_b1[j], conv_dw[j], conv_dw_b[j], conv_ln_g[j], conv_ln_b[j],
                      conv_w2[j], conv_b2[j])
            oc = conformer_conv(hc, *conv_p)
            ol = conformer_conv(hl, *conv_p)
        else:
            rwkv_p = (rwkv_mu[j], rwkv_wr[j], rwkv_wk[j], rwkv_wv[j], rwkv_wo[j], rwkv_w0[j], rwkv_w1[j],
                      rwkv_w2[j], rwkv_a0[j], rwkv_a1[j], rwkv_a2[j], rwkv_g1[j], rwkv_g2[j], rwkv_k_k[j],
                      rwkv_k_a[j], rwkv_r_k[j], rwkv_lnx_g[j], rwkv_lnx_b[j])
            s_zero = jnp.zeros((xc.shape[0], 2, RWKV_HEADS, RWKV_HEAD, RWKV_HEAD), jnp.float32)
            oc, s_ctx = rwkv7_bidir(hc, s_zero, *rwkv_p)
            ol, _ = rwkv7_bidir(hl, state_wkv[:, j], *rwkv_p)
            new_s.append(s_ctx)
        xc = xc + gt1c * oc
        xl = xl + gt1l * ol
        xc = xc + gt2c * peer_mixer(modulate(xc, norm2[i], sh2c, sc2c), peer_wq[i], peer_keys[i], peer_u[i], peer_v[i])
        xl = xl + gt2l * peer_mixer(modulate(xl, norm2[i], sh2l, sc2l), peer_wq[i], peer_keys[i], peer_u[i], peer_v[i])
    y_prompt = rms_norm(xc, final_norm)
    y_sample = rms_norm(xl, final_norm)
    return (y_prompt, y_sample, jnp.stack(new_k, axis=1), jnp.stack(new_v, axis=1), jnp.stack(new_s, axis=1))
```

```python
import collections
import functools
import math

import jax
import jax.numpy as jnp
from jax import lax
from jax.experimental import pallas as pl
from jax.experimental.pallas import tpu as pltpu

F32 = jnp.float32
BF16 = jnp.bfloat16
HIGHEST = lax.Precision.HIGHEST

HEAD_DIM = 128
KV_GROUP = 4
ROPE_THETA = 10000.0
GRID_W = 64
CONV_WIDTH = 31
RWKV_HEAD = 64
GN_EPS = 64e-5
N_KEYS = 128
PEER_HEADS = 8
PEER_TOPK = 16
NORM_EPS = 1e-6
LN_EPS = 1e-5

LANES = 128
SUBLANES = 8
VMEM_LIMIT_BYTES = 56 * 1024 * 1024

WKV_CHUNK = 64
WKV_GROUP = 4

Layout = collections.namedtuple("Layout", "n_ctx ctx_seq n_lat lat_seq")


def _ctx_rows(lay):
    return lay.n_ctx * lay.ctx_seq


def _rows(lay):
    return _ctx_rows(lay) + lay.n_lat * lay.lat_seq


def _cond_of_tile(i, tm, lay):
    n_ctx_tiles = _ctx_rows(lay) // tm
    per = lay.lat_seq // tm
    return jnp.where(i < n_ctx_tiles, 0, 1 + (i - n_ctx_tiles) // per)


def _params(*sem):
    return pltpu.CompilerParams(dimension_semantics=sem, vmem_limit_bytes=VMEM_LIMIT_BYTES)


def _tile(n, pref):
    t = min(n, pref)
    while n % t:
        t //= 2
    assert t == n or t % LANES == 0, (n, pref)
    return t


def _mod_spec(which, tm, lay, d, ngrid):
    if ngrid == 1:
        return pl.BlockSpec((1, 1, d), lambda i: (_cond_of_tile(i, tm, lay) * 6 + which, 0, 0))
    return pl.BlockSpec((1, 1, d), lambda i, j: (_cond_of_tile(i, tm, lay) * 6 + which, 0, 0))


def _rms_mod(x, g, shift, scale):
    ms = jnp.mean(x * x, axis=-1, keepdims=True)
    return (x * lax.rsqrt(ms + NORM_EPS) * g) * (1.0 + scale) + shift


def _modulate_kernel(x_ref, g_ref, sh_ref, sc_ref, o_ref, *, transpose):
    h = _rms_mod(x_ref[...], g_ref[...], sh_ref[0], sc_ref[0])
    if transpose:
        o_ref[...] = h.T.astype(o_ref.dtype)
    else:
        o_ref[...] = h.astype(o_ref.dtype)


def modulate(x, g, mods, which_shift, lay, *, out_dtype=BF16, transpose=False, tm=256):
    t, d = x.shape
    tm = _tile(math.gcd(lay.ctx_seq * lay.n_ctx, lay.lat_seq), tm)
    out_shape = (d, t) if transpose else (t, d)
    out_spec = pl.BlockSpec((d, tm), lambda i: (0, i)) if transpose else pl.BlockSpec((tm, d), lambda i: (i, 0))
    return pl.pallas_call(
        functools.partial(_modulate_kernel, transpose=transpose),
        grid=(t // tm,),
        in_specs=[pl.BlockSpec((tm, d), lambda i: (i, 0)),
                  pl.BlockSpec((1, d), lambda i: (0, 0)),
                  _mod_spec(which_shift, tm, lay, d, 1),
                  _mod_spec(which_shift + 1, tm, lay, d, 1)],
        out_specs=out_spec,
        out_shape=jax.ShapeDtypeStruct(out_shape, out_dtype),
        compiler_params=_params("parallel"),
        name="modulate_t" if transpose else "modulate",
    )(x, g.reshape(1, d), mods, mods)


def _linear_kernel(*refs, has_bias, act, glu, resid):
    it = iter(refs)
    a_ref = next(it)
    w_ref = next(it)
    w2_ref = next(it) if glu else None
    b_ref = next(it) if has_bias else None
    b2_ref = next(it) if (glu and has_bias) else None
    x_ref = next(it) if resid else None
    gt_ref = next(it) if resid else None
    o_ref = next(it)
    a = a_ref[...].astype(BF16)
    acc = jnp.dot(a, w_ref[...], preferred_element_type=F32)
    if has_bias:
        acc = acc + b_ref[...]
    if glu:
        acc2 = jnp.dot(a, w2_ref[...], preferred_element_type=F32)
        if has_bias:
            acc2 = acc2 + b2_ref[...]
        acc = acc * jax.nn.sigmoid(acc2)
    if act == "sigmoid":
        acc = jax.nn.sigmoid(acc)
    elif act == "tanh":
        acc = jnp.tanh(acc)
    if resid:
        acc = x_ref[...] + gt_ref[0] * acc
    o_ref[...] = acc.astype(o_ref.dtype)


def linear(a, w, *, bias=None, act=None, glu=False, resid=None, out_dtype=F32, tm=1024, tn=512, name="linear"):
    m, k = a.shape
    n = w.shape[1] // 2 if glu else w.shape[1]
    if resid is not None:
        tm = _tile(math.gcd(_ctx_rows(resid[3]), resid[3].lat_seq), tm)
    tm = _tile(m, tm)
    tn = _tile(n, tn)
    nj = n // tn
    in_specs = [pl.BlockSpec((tm, k), lambda i, j: (i, 0)),
                pl.BlockSpec((k, tn), lambda i, j: (0, j))]
    args = [a, w]
    if glu:
        in_specs.append(pl.BlockSpec((k, tn), lambda i, j: (0, j + nj)))
        args.append(w)
    if bias is not None:
        b2d = bias.reshape(1, -1).astype(F32)
        in_specs.append(pl.BlockSpec((1, tn), lambda i, j: (0, j)))
        args.append(b2d)
        if glu:
            in_specs.append(pl.BlockSpec((1, tn), lambda i, j: (0, j + nj)))
            args.append(b2d)
    if resid is not None:
        x, mods, which, lay = resid
        in_specs.append(pl.BlockSpec((tm, tn), lambda i, j: (i, j)))
        args.append(x)
        in_specs.append(pl.BlockSpec((1, 1, tn), lambda i, j: (_cond_of_tile(i, tm, lay) * 6 + which, 0, j)))
        args.append(mods)
    return pl.pallas_call(
        functools.partial(_linear_kernel, has_bias=bias is not None, act=act, glu=glu, resid=resid is not None),
        grid=(m // tm, nj),
        in_specs=in_specs,
        out_specs=pl.BlockSpec((tm, tn), lambda i, j: (i, j)),
        out_shape=jax.ShapeDtypeStruct((m, n), out_dtype),
        compiler_params=_params("parallel", "parallel"),
        name=name,
    )(*args)


def rope_tables(lat_seq):
    rows = lat_seq // GRID_W
    pos_row = jnp.repeat(jnp.arange(rows, dtype=F32), GRID_W)
    pos_col = jnp.tile(jnp.arange(GRID_W, dtype=F32), rows)
    n_freq = HEAD_DIM // 4
    inv = ROPE_THETA ** (-jnp.arange(n_freq, dtype=F32) / n_freq)
    ang_r = pos_row[:, None] * inv
    ang_c = pos_col[:, None] * inv
    cos = jnp.concatenate([jnp.cos(ang_r), jnp.cos(ang_r), jnp.cos(ang_c), jnp.cos(ang_c)], axis=1)
    sin = jnp.concatenate([-jnp.sin(ang_r), jnp.sin(ang_r), -jnp.sin(ang_c), jnp.sin(ang_c)], axis=1)
    return cos, sin


def _head_rms(x, g):
    ms = jnp.mean(x * x, axis=-1, keepdims=True)
    return x * lax.rsqrt(ms + NORM_EPS) * g


def _rope(x, cos, sin):
    q = HEAD_DIM // 4
    lane = lax.broadcasted_iota(jnp.int32, x.shape, 1)
    first = (lane // q) % 2 == 0
    partner = jnp.where(first, pltpu.roll(x, HEAD_DIM - q, 1), pltpu.roll(x, q, 1))
    return x * cos + partner * sin


def _qkv_post_kernel(*refs, n_heads, n_kv, rope, emit_f32):
    it = iter(refs)
    qkv_ref = next(it)
    qn_ref = next(it)
    kn_ref = next(it)
    cos_ref = next(it) if rope else None
    sin_ref = next(it) if rope else None
    q_ref = next(it)
    k_ref = next(it)
    v_ref = next(it)
    kf_ref = next(it) if emit_f32 else None
    vf_ref = next(it) if emit_f32 else None
    for h in range(n_heads + n_kv):
        x = qkv_ref[:, h * HEAD_DIM:(h + 1) * HEAD_DIM]
        is_q = h < n_heads
        y = _head_rms(x, qn_ref[...] if is_q else kn_ref[...])
        if emit_f32 and not is_q:
            kf_ref[:, (h - n_heads) * HEAD_DIM:(h - n_heads + 1) * HEAD_DIM] = y
        if rope:
            y = _rope(y, cos_ref[...], sin_ref[...])
        if is_q:
            q_ref[:, h * HEAD_DIM:(h + 1) * HEAD_DIM] = y.astype(q_ref.dtype)
        else:
            k_ref[:, (h - n_heads) * HEAD_DIM:(h - n_heads + 1) * HEAD_DIM] = y.astype(k_ref.dtype)
    v = qkv_ref[:, (n_heads + n_kv) * HEAD_DIM:]
    v_ref[...] = v.astype(v_ref.dtype)
    if emit_f32:
        vf_ref[...] = v


def qkv_post(qkv, q_norm, k_norm, *, row0, nrows, seq, rope_tabs, emit_f32, n_heads, tm=256):
    n_kv = n_heads // KV_GROUP
    dq, dk = n_heads * HEAD_DIM, n_kv * HEAD_DIM
    tm = _tile(seq, tm)
    assert row0 % tm == 0
    r0 = row0 // tm
    rope = rope_tabs is not None
    in_specs = [pl.BlockSpec((tm, dq + 2 * dk), lambda i: (i + r0, 0)),
                pl.BlockSpec((1, HEAD_DIM), lambda i: (0, 0)),
                pl.BlockSpec((1, HEAD_DIM), lambda i: (0, 0))]
    args = [qkv, q_norm.reshape(1, HEAD_DIM), k_norm.reshape(1, HEAD_DIM)]
    if rope:
        per = seq // tm
        in_specs += [pl.BlockSpec((tm, HEAD_DIM), lambda i: (i % per, 0))] * 2
        args += list(rope_tabs)
    out_shape = [jax.ShapeDtypeStruct((nrows, dq), BF16), jax.ShapeDtypeStruct((nrows, dk), BF16),
                 jax.ShapeDtypeStruct((nrows, dk), BF16)]
    out_specs = [pl.BlockSpec((tm, dq), lambda i: (i, 0)), pl.BlockSpec((tm, dk), lambda i: (i, 0)),
                 pl.BlockSpec((tm, dk), lambda i: (i, 0))]
    if emit_f32:
        out_shape += [jax.ShapeDtypeStruct((nrows, dk), F32)] * 2
        out_specs += [pl.BlockSpec((tm, dk), lambda i: (i, 0))] * 2
    return pl.pallas_call(
        functools.partial(_qkv_post_kernel, n_heads=n_heads, n_kv=n_kv, rope=rope, emit_f32=emit_f32),
        grid=(nrows // tm,),
        in_specs=in_specs, out_specs=out_specs, out_shape=out_shape,
        compiler_params=_params("parallel"),
        name="qkv_post_rope" if rope else "qkv_post",
    )(*args)


def _attn_kernel(*refs, has_cache):
    it = iter(refs)
    q_ref = next(it)
    k_ref = next(it)
    v_ref = next(it)
    kc_ref = next(it) if has_cache else None
    vc_ref = next(it) if has_cache else None
    o_ref = next(it)
    scale = HEAD_DIM ** -0.5
    nt = (((1,), (1,)), ((), ()))
    k = k_ref[...]
    v = v_ref[...]
    for h in range(KV_GROUP):
        q = q_ref[:, h * HEAD_DIM:(h + 1) * HEAD_DIM]
        s = lax.dot_general(q, k, nt, preferred_element_type=F32) * scale
        m = jnp.max(s, axis=-1, keepdims=True)
        if has_cache:
            s2 = lax.dot_general(q, kc_ref[0], nt, preferred_element_type=F32) * scale
            m = jnp.maximum(m, jnp.max(s2, axis=-1, keepdims=True))
        p = jnp.exp(s - m)
        l = jnp.sum(p, axis=-1, keepdims=True)
        o = jnp.dot(p.astype(BF16), v, preferred_element_type=F32)
        if has_cache:
            p2 = jnp.exp(s2 - m)
            l = l + jnp.sum(p2, axis=-1, keepdims=True)
            o = o + jnp.dot(p2.astype(BF16), vc_ref[0], preferred_element_type=F32)
        o_ref[:, h * HEAD_DIM:(h + 1) * HEAD_DIM] = (o / l).astype(o_ref.dtype)


def attention(q, k, v, cache_k, cache_v, *, nbatch, seq, tq=256):
    n_kv = k.shape[1] // HEAD_DIM
    tq = _tile(seq, tq)
    nq = seq // tq
    gw = KV_GROUP * HEAD_DIM
    has_cache = cache_k is not None
    in_specs = [pl.BlockSpec((tq, gw), lambda b, g, i: (b * nq + i, g)),
                pl.BlockSpec((seq, HEAD_DIM), lambda b, g, i: (b, g)),
                pl.BlockSpec((seq, HEAD_DIM), lambda b, g, i: (b, g))]
    args = [q, k, v]
    if has_cache:
        past = cache_k.shape[1]
        in_specs += [pl.BlockSpec((1, past, HEAD_DIM), lambda b, g, i: (b, 0, g))] * 2
        args += [cache_k, cache_v]
    return pl.pallas_call(
        functools.partial(_attn_kernel, has_cache=has_cache),
        grid=(nbatch, n_kv, nq),
        in_specs=in_specs,
        out_specs=pl.BlockSpec((tq, gw), lambda b, g, i: (b * nq + i, g)),
        out_shape=jax.ShapeDtypeStruct(q.shape, BF16),
        compiler_params=_params("parallel", "parallel", "parallel"),
        name="attention_cache" if has_cache else "attention",
    )(*args)


def _seq_edges(i, tm, lay):
    n_ctx_tiles = _ctx_rows(lay) // tm
    per_c = lay.ctx_seq // tm
    per_l = lay.lat_seq // tm
    il = i - n_ctx_tiles
    first = jnp.where(i < n_ctx_tiles, i % per_c == 0, il % per_l == 0)
    last = jnp.where(i < n_ctx_tiles, i % per_c == per_c - 1, il % per_l == per_l - 1)
    return first, last


def _halo_specs(tm, halo, d, nblocks_total):
    r = tm // halo
    prev = pl.BlockSpec((halo, d), lambda i: (jnp.maximum(i * r - 1, 0), 0))
    nxt = pl.BlockSpec((halo, d), lambda i: (jnp.minimum((i + 1) * r, nblocks_total - 1), 0))
    return prev, nxt


CONV_HALO = 16


def _dwconv_kernel(u_ref, up_ref, un_ref, dw_ref, dwb_ref, g_ref, b_ref, o_ref, ext_ref, acc_ref, *, tm, lay, cb):
    i = pl.program_id(0)
    first, last = _seq_edges(i, tm, lay)
    d = u_ref.shape[1]
    ext_ref[CONV_HALO:CONV_HALO + tm, :] = u_ref[...]
    ext_ref[0:CONV_HALO, :] = jnp.where(first, 0.0, up_ref[...])
    ext_ref[CONV_HALO + tm:, :] = jnp.where(last, 0.0, un_ref[...])
    pad = CONV_WIDTH // 2

    def col_block(c, carry):
        cs = pl.multiple_of(c * cb, cb)
        w = ext_ref[:, pl.ds(cs, cb)]
        taps = dw_ref[:, pl.ds(cs, cb)]
        acc = jnp.zeros((tm, cb), F32)
        for j in range(CONV_WIDTH):
            o = CONV_HALO - pad + j
            acc = acc + w[o:o + tm] * taps[j:j + 1]
        acc_ref[:, pl.ds(cs, cb)] = acc
        return carry

    lax.fori_loop(0, d // cb, col_block, 0)
    y = acc_ref[...] + dwb_ref[...]
    mu = jnp.mean(y, axis=-1, keepdims=True)
    yc = y - mu
    var = jnp.mean(yc * yc, axis=-1, keepdims=True)
    z = yc * lax.rsqrt(var + LN_EPS) * g_ref[...] + b_ref[...]
    o_ref[...] = (z * jax.nn.sigmoid(z)).astype(o_ref.dtype)


def dwconv_ln_silu(u, dw, dw_b, ln_g, ln_b, lay, *, tm=128, cb=256):
    t, d = u.shape
    tm = _tile(math.gcd(lay.ctx_seq, lay.lat_seq), tm)
    cb = _tile(d, cb)
    prev, nxt = _halo_specs(tm, CONV_HALO, d, t // CONV_HALO)
    dwp = jnp.zeros((32, d), F32).at[:CONV_WIDTH].set(dw)
    row = lambda a: a.reshape(1, d)
    const = lambda r: pl.BlockSpec((r, d), lambda i: (0, 0))
    return pl.pallas_call(
        functools.partial(_dwconv_kernel, tm=tm, lay=lay, cb=cb),
        grid=(t // tm,),
        in_specs=[pl.BlockSpec((tm, d), lambda i: (i, 0)), prev, nxt,
                  const(32), const(1), const(1), const(1)],
        out_specs=pl.BlockSpec((tm, d), lambda i: (i, 0)),
        out_shape=jax.ShapeDtypeStruct((t, d), BF16),
        scratch_shapes=[pltpu.VMEM((tm + 2 * CONV_HALO, d), F32), pltpu.VMEM((tm, d), F32)],
        compiler_params=_params("parallel"),
        name="dwconv_ln_silu",
    )(u, u, u, dwp, row(dw_b), row(ln_g), row(ln_b))


MIX_HALO = 8


def _rwkv_mix_kernel(x_ref, xp_ref, xn_ref, g_ref, sh_ref, sc_ref, mu_ref, *o_refs, tm, lay):
    i = pl.program_id(0)
    first, last = _seq_edges(i, tm, lay)
    g, sh, sc = g_ref[...], sh_ref[0], sc_ref[0]
    h = _rms_mod(x_ref[...], g, sh, sc)
    hp = _rms_mod(xp_ref[MIX_HALO - 1:MIX_HALO, :], g, sh, sc)
    hn = _rms_mod(xn_ref[0:1, :], g, sh, sc)
    hp = jnp.where(first, 0.0, hp)
    hn = jnp.where(last, 0.0, hn)
    row = lax.broadcasted_iota(jnp.int32, h.shape, 0)
    h_prev = jnp.where(row == 0, hp, pltpu.roll(h, 1, 0))
    h_next = jnp.where(row == tm - 1, hn, pltpu.roll(h, tm - 1, 0))
    xx = 0.5 * (h_prev + h_next) - h
    for n, o_ref in enumerate(o_refs):
        o_ref[...] = (h + xx * mu_ref[n:n + 1, :]).astype(o_ref.dtype)


def rwkv_mix(x, g, mods, mu, lay, *, tm=256):
    t, d = x.shape
    tm = _tile(math.gcd(lay.ctx_seq, lay.lat_seq), tm)
    prev, nxt = _halo_specs(tm, MIX_HALO, d, t // MIX_HALO)
    mup = jnp.zeros((8, d), F32).at[:6].set(mu)
    return pl.pallas_call(
        functools.partial(_rwkv_mix_kernel, tm=tm, lay=lay),
        grid=(t // tm,),
        in_specs=[pl.BlockSpec((tm, d), lambda i: (i, 0)), prev, nxt,
                  pl.BlockSpec((1, d), lambda i: (0, 0)),
                  _mod_spec(0, tm, lay, d, 1), _mod_spec(1, tm, lay, d, 1),
                  pl.BlockSpec((8, d), lambda i: (0, 0))],
        out_specs=[pl.BlockSpec((tm, d), lambda i: (i, 0))] * 6,
        out_shape=[jax.ShapeDtypeStruct((t, d), BF16)] * 6,
        compiler_params=_params("parallel"),
        name="rwkv_mix",
    )(x, x, x, g.reshape(1, d), mods, mods, mup)


def _head_sum(x, ones_bd):
    return jnp.dot(x, ones_bd, precision=HIGHEST, preferred_element_type=F32)


def _wkv_kernel(r_ref, k_ref, v_ref, wl_ref, al_ref, kkp_ref, ka_ref, rk_ref, s0_ref,
                y_ref, bo_ref, sf_ref, s_scr, *, reverse, nc, prec):
    C = WKV_CHUNK
    N = RWKV_HEAD
    W = WKV_GROUP * N
    c = pl.program_id(2)

    @pl.when(c == 0)
    def _():
        s_scr[...] = s0_ref[0, 0]

    lane_i = lax.broadcasted_iota(jnp.int32, (W, W), 0)
    lane_j = lax.broadcasted_iota(jnp.int32, (W, W), 1)
    bd_mask = (lane_i // N) == (lane_j // N)
    ones_bd = bd_mask.astype(F32)

    r = r_ref[...]
    k = k_ref[...]
    v = v_ref[...]
    iclr = jax.nn.sigmoid(al_ref[...])
    z = -wl_ref[...]
    softplus = jnp.maximum(z, 0.0) + jnp.log1p(jnp.exp(-jnp.abs(z)))
    lw = -jnp.exp(-softplus - 0.5)
    kx = k * kkp_ref[...]
    kk = kx / jnp.maximum(jnp.sqrt(_head_sum(kx * kx, ones_bd)), 1e-12)
    kd = k * (1.0 + (iclr - 1.0) * ka_ref[...])
    a = -kk
    b = kk * iclr
    bo_ref[...] = _head_sum(r * kd * rk_ref[...], ones_bd) * v

    tt = lax.broadcasted_iota(jnp.int32, (C, C), 0)
    ss = lax.broadcasted_iota(jnp.int32, (C, C), 1)
    tri = (ss >= tt) if reverse else (ss <= tt)
    cum = jnp.dot(tri.astype(F32), lw, precision=HIGHEST, preferred_element_type=F32)
    g_in = jnp.exp(cum)
    g_ex = jnp.exp(cum - lw)
    g_inv = jnp.exp(-cum)
    at = a * g_ex
    rt = r * g_in
    bt = b * g_inv
    kt = kd * g_inv

    def bd(x):
        return jnp.where(bd_mask, jnp.concatenate([x] * WKV_GROUP, axis=0), 0.0)

    nt = (((1,), (1,)), ((), ()))
    dot_nt = lambda x, y: lax.dot_general(x, y, nt, precision=prec, preferred_element_type=F32)
    dot_nn = lambda x, y: jnp.dot(x, y, precision=prec, preferred_element_type=F32)

    t2 = lax.broadcasted_iota(jnp.int32, (C, W), 0)
    s2 = lax.broadcasted_iota(jnp.int32, (C, W), 1) % C
    strict = (s2 > t2) if reverse else (s2 < t2)
    incl = (s2 >= t2) if reverse else (s2 <= t2)
    eye = (s2 == t2).astype(F32)

    s_old = s_scr[...]
    ar = jnp.concatenate([at, rt], axis=0)
    p_b = dot_nt(ar, bd(bt))
    p_k = dot_nt(ar, bd(kt))
    p_s = dot_nt(ar, bd(s_old))
    a_ab = jnp.where(strict, p_b[:C], 0.0)
    a_ak = jnp.where(strict, p_k[:C], 0.0)
    a_rb = jnp.where(incl, p_b[C:], 0.0)
    a_rk = jnp.where(incl, p_k[C:], 0.0)
    bdv = bd(v)
    rhs = p_s[:C] + dot_nn(a_ak, bdv)
    lp = a_ab
    tinv = eye + lp
    for _ in range(int(math.log2(C)) - 1):
        lp = dot_nn(lp, bd(lp))
        tinv = tinv + dot_nn(tinv, bd(lp))
    u = dot_nn(tinv, bd(rhs))
    y_ref[...] = p_s[C:] + dot_nn(a_rb, bd(u)) + dot_nn(a_rk, bdv)

    uv = jnp.concatenate([u, v], axis=0)
    bk = jnp.concatenate([bt, kt], axis=0)
    f = jnp.where(bd_mask, dot_nn(uv.T, bk), 0.0)
    zsum = f[0:N]
    for h in range(1, WKV_GROUP):
        zsum = zsum + f[h * N:(h + 1) * N]
    g_end = g_in[0:1] if reverse else g_in[C - 1:C]
    s_new = (s_old + zsum) * g_end
    s_scr[...] = s_new

    @pl.when(c == nc - 1)
    def _():
        sf_ref[0, 0] = s_new


def wkv_scan(r, k, v, wl, al, k_k, k_a, r_k, s0, *, row0, nbatch, seq, reverse, prec=HIGHEST):
    d = r.shape[1]
    C, W = WKV_CHUNK, WKV_GROUP * RWKV_HEAD
    assert seq % C == 0 and row0 % C == 0 and d % W == 0
    nc, ng, rb0 = seq // C, d // W, row0 // C

    def chunk(c):
        return (nc - 1 - c) if reverse else c

    tok_in = pl.BlockSpec((C, W), lambda b, g, c: (rb0 + b * nc + chunk(c), g))
    tok_out = pl.BlockSpec((C, W), lambda b, g, c: (b * nc + chunk(c), g))
    par = pl.BlockSpec((1, W), lambda b, g, c: (0, g))
    st = pl.BlockSpec((1, 1, RWKV_HEAD, W), lambda b, g, c: (b, g, 0, 0))
    row = lambda x: x.reshape(1, d)
    return pl.pallas_call(
        functools.partial(_wkv_kernel, reverse=reverse, nc=nc, prec=prec),
        grid=(nbatch, ng, nc),
        in_specs=[tok_in] * 5 + [par] * 3 + [st],
        out_specs=[tok_out, tok_out, st],
        out_shape=[jax.ShapeDtypeStruct((nbatch * seq, d), F32), jax.ShapeDtypeStruct((nbatch * seq, d), F32),
                   jax.ShapeDtypeStruct(s0.shape, F32)],
        scratch_shapes=[pltpu.VMEM((RWKV_HEAD, W), F32)],
        compiler_params=_params("parallel", "parallel", "arbitrary"),
        name="wkv_scan_rev" if reverse else "wkv_scan_fwd",
    )(r, k, v, wl, al, row(k_k), row(k_a), row(r_k), s0)


def _rwkv_out_kernel(y0_ref, y1_ref, b0_ref, b1_ref, gate_ref, g_ref, b_ref, o_ref):
    d = y0_ref.shape[1]
    li = lax.broadcasted_iota(jnp.int32, (LANES, LANES), 0)
    lj = lax.broadcasted_iota(jnp.int32, (LANES, LANES), 1)
    avg = ((li // RWKV_HEAD) == (lj // RWKV_HEAD)).astype(F32) * (1.0 / RWKV_HEAD)
    for c in range(d // LANES):
        sl = slice(c * LANES, (c + 1) * LANES)
        y = y0_ref[:, sl] + y1_ref[:, sl]
        mu = jnp.dot(y, avg, precision=HIGHEST, preferred_element_type=F32)
        yc = y - mu
        var = jnp.dot(yc * yc, avg, precision=HIGHEST, preferred_element_type=F32)
        yn = yc * lax.rsqrt(var + GN_EPS) * g_ref[:, sl] + b_ref[:, sl]
        o_ref[:, sl] = ((yn + b0_ref[:, sl] + b1_ref[:, sl]) * gate_ref[:, sl]).astype(o_ref.dtype)


def rwkv_out(y0, y1, b0, b1, gate, lnx_g, lnx_b, *, tm=256):
    t, d = y0.shape
    tm = _tile(t, tm)
    tok = pl.BlockSpec((tm, d), lambda i: (i, 0))
    par = pl.BlockSpec((1, d), lambda i: (0, 0))
    return pl.pallas_call(
        _rwkv_out_kernel,
        grid=(t // tm,),
        in_specs=[tok] * 5 + [par] * 2,
        out_specs=tok,
        out_shape=jax.ShapeDtypeStruct((t, d), BF16),
        compiler_params=_params("parallel"),
        name="rwkv_out",
    )(y0, y1, b0, b1, gate, lnx_g.reshape(1, d), lnx_b.reshape(1, d))


def _extract_desc(x, n):
    vals = []
    cur = x
    for _ in range(n):
        mx = jnp.max(cur, axis=0, keepdims=True)
        vals.append(jnp.maximum(mx, 0.0))
        cur = jnp.where(cur == mx, -1.0, cur)
    return vals


def _peer_route_kernel(q_ref, keys_ref, ea_ref, e1_ref, th_ref):
    kp = PEER_TOPK
    half = kp // 2
    ths = []
    for h in range(PEER_HEADS):
        es = []
        tops = []
        for p in range(2):
            idx = h * 2 + p
            qhp = q_ref[idx * N_KEYS:(idx + 1) * N_KEYS, :]
            s = jnp.dot(keys_ref[idx], qhp, precision=HIGHEST, preferred_element_type=F32)
            e = jnp.exp(s - jnp.max(s, axis=0, keepdims=True))
            es.append(e)
            tops.append(jnp.concatenate(_extract_desc(e, kp), axis=0))
        v0, v1 = tops

        def cands(a0):
            blk = [a0[a:a + 1] * v1[:half] for a in range(half)]
            blk.append(a0[0:1] * v1[half:])
            blk.append(a0[half:] * v1[0:1])
            return jnp.concatenate(blk, axis=0)

        cand = cands(v0)
        best = _extract_desc(cand, kp)
        zsum = best[0]
        for b in best[1:]:
            zsum = zsum + b
        inv_z = 1.0 / zsum
        cand_n = cands(v0 * inv_z)
        th = jnp.min(jnp.where(cand >= best[kp - 1], cand_n, jnp.inf), axis=0, keepdims=True)
        ths.append(th)
        ea_ref[h * N_KEYS:(h + 1) * N_KEYS, :] = es[0] * inv_z
        e1_ref[h * N_KEYS:(h + 1) * N_KEYS, :] = es[1]
    th_ref[...] = jnp.concatenate(ths, axis=0)


def peer_route(q_t, keys, *, tn=256):
    rows, t = q_t.shape
    tn = _tile(t, tn)
    hk = PEER_HEADS * N_KEYS
    return pl.pallas_call(
        _peer_route_kernel,
        grid=(t // tn,),
        in_specs=[pl.BlockSpec((rows, tn), lambda i: (0, i)),
                  pl.BlockSpec(keys.shape, lambda i: (0, 0, 0))],
        out_specs=[pl.BlockSpec((hk, tn), lambda i: (0, i)), pl.BlockSpec((hk, tn), lambda i: (0, i)),
                   pl.BlockSpec((PEER_HEADS, tn), lambda i: (0, i))],
        out_shape=[jax.ShapeDtypeStruct((hk, t), F32), jax.ShapeDtypeStruct((hk, t), F32),
                   jax.ShapeDtypeStruct((PEER_HEADS, t), F32)],
        compiler_params=_params("parallel"),
        name="peer_route",
    )(q_t, keys)


def _peer_dense_kernel(h_ref, u_ref, vt_ref, ea_ref, e1_ref, th_ref, x_ref, gt_ref, o_ref, acc_ref, w_ref, *, ni):
    e = pl.program_id(1)

    @pl.when(e == 0)
    def _():
        acc_ref[...] = jnp.zeros_like(acc_ref)

    tn = h_ref.shape[1]
    for ii in range(ni):
        w = jnp.zeros((N_KEYS, tn), F32)
        for h in range(PEER_HEADS):
            row = ea_ref[pl.ds(h * N_KEYS + e * ni + ii, 1), :]
            p = row * e1_ref[h * N_KEYS:(h + 1) * N_KEYS, :]
            w = w + jnp.where(p >= th_ref[h:h + 1, :], p, 0.0)
        w_ref[ii * N_KEYS:(ii + 1) * N_KEYS, :] = w
    a = jnp.dot(u_ref[...], h_ref[...], preferred_element_type=F32)
    act = 0.5 * a * (1.0 + lax.erf(a * (2.0 ** -0.5)))
    gw = (w_ref[...] * act).astype(BF16)
    acc_ref[...] += jnp.dot(vt_ref[...], gw, preferred_element_type=F32)

    @pl.when(e == pl.num_programs(1) - 1)
    def _():
        o_ref[...] = x_ref[...] + gt_ref[0] * acc_ref[...].T


def peer_dense(h_t, u_tab, vt_tab, ea, e1, th, x, mods, which_gate, lay, *, tn=512, te=1024):
    d, t = h_t.shape
    n_exp = u_tab.shape[0]
    tn = _tile(math.gcd(_ctx_rows(lay), lay.lat_seq), tn)
    te = _tile(n_exp, te)
    ni = te // N_KEYS
    hk = PEER_HEADS * N_KEYS
    return pl.pallas_call(
        functools.partial(_peer_dense_kernel, ni=ni),
        grid=(t // tn, n_exp // te),
        in_specs=[pl.BlockSpec((d, tn), lambda i, e: (0, i)),
                  pl.BlockSpec((te, d), lambda i, e: (e, 0)),
                  pl.BlockSpec((d, te), lambda i, e: (0, e)),
                  pl.BlockSpec((hk, tn), lambda i, e: (0, i)),
                  pl.BlockSpec((hk, tn), lambda i, e: (0, i)),
                  pl.BlockSpec((PEER_HEADS, tn), lambda i, e: (0, i)),
                  pl.BlockSpec((tn, d), lambda i, e: (i, 0)),
                  pl.BlockSpec((1, 1, d), lambda i, e: (_cond_of_tile(i, tn, lay) * 6 + which_gate, 0, 0))],
        out_specs=pl.BlockSpec((tn, d), lambda i, e: (i, 0)),
        out_shape=jax.ShapeDtypeStruct((t, d), F32),
        scratch_shapes=[pltpu.VMEM((d, tn), F32), pltpu.VMEM((te, tn), F32)],
        compiler_params=_params("parallel", "arbitrary"),
        name="peer_dense",
    )(h_t, u_tab, vt_tab, ea, e1, th, x, mods)


def _rmsnorm_kernel(x_ref, g_ref, o_ref):
    x = x_ref[...]
    ms = jnp.mean(x * x, axis=-1, keepdims=True)
    o_ref[...] = x * lax.rsqrt(ms + NORM_EPS) * g_ref[...]


def rmsnorm(x, g, *, tm=512):
    t, d = x.shape
    tm = _tile(t, tm)
    return pl.pallas_call(
        _rmsnorm_kernel,
        grid=(t // tm,),
        in_specs=[pl.BlockSpec((tm, d), lambda i: (i, 0)), pl.BlockSpec((1, d), lambda i: (0, 0))],
        out_specs=pl.BlockSpec((tm, d), lambda i: (i, 0)),
        out_shape=jax.ShapeDtypeStruct((t, d), F32),
        compiler_params=_params("parallel"),
        name="final_rmsnorm",
    )(x, g.reshape(1, d))


def _pad_cols(w, n):
    return jnp.zeros(w.shape[:-1] + (n,), w.dtype).at[..., :w.shape[-1]].set(w)


def _pad_rows(w, n):
    return jnp.zeros((n,) + w.shape[1:], w.dtype).at[:w.shape[0]].set(w)


def _state_to_groups(s):
    b, h, n, _ = s.shape
    return s.reshape(b, h // WKV_GROUP, WKV_GROUP, n, n).transpose(0, 1, 3, 2, 4).reshape(b, h // WKV_GROUP, n, WKV_GROUP * n)


def _groups_to_state(s):
    b, g, n, _ = s.shape
    return s.reshape(b, g, n, WKV_GROUP, n).transpose(0, 1, 3, 2, 4).reshape(b, g * WKV_GROUP, n, n)


def attention_layer(x, mods, lay, norm_g, wqkv, q_norm, k_norm, wo, cache_k, cache_v):
    t, d = x.shape
    n_heads = d // HEAD_DIM
    n_kv = n_heads // KV_GROUP
    h = modulate(x, norm_g, mods, 0, lay)
    qkv = linear(h, wqkv.astype(BF16), name="attn_qkv")
    nc = _ctx_rows(lay)
    qc, kc, vc, kcf, vcf = qkv_post(qkv, q_norm, k_norm, row0=0, nrows=nc, seq=lay.ctx_seq,
                                    rope_tabs=None, emit_f32=True, n_heads=n_heads)
    ql, kl, vl = qkv_post(qkv, q_norm, k_norm, row0=nc, nrows=t - nc, seq=lay.lat_seq,
                          rope_tabs=rope_tables(lay.lat_seq), emit_f32=False, n_heads=n_heads)
    oc = attention(qc, kc, vc, None, None, nbatch=lay.n_ctx, seq=lay.ctx_seq)
    past = cache_k.shape[1]
    ck = cache_k.reshape(lay.n_lat, past, n_kv * HEAD_DIM).astype(BF16)
    cv = cache_v.reshape(lay.n_lat, past, n_kv * HEAD_DIM).astype(BF16)
    ol = attention(ql, kl, vl, ck, cv, nbatch=lay.n_lat, seq=lay.lat_seq)
    o = jnp.concatenate([oc, ol], axis=0)
    x = linear(o, wo.astype(BF16), resid=(x, mods, 2, lay), name="attn_out")
    new_k = kcf.reshape(lay.n_ctx, lay.ctx_seq, n_kv, HEAD_DIM)
    new_v = vcf.reshape(lay.n_ctx, lay.ctx_seq, n_kv, HEAD_DIM)
    return x, new_k, new_v


def conv_layer(x, mods, lay, norm_g, w1, b1, dw, dw_b, ln_g, ln_b, w2, b2):
    h = modulate(x, norm_g, mods, 0, lay)
    u = linear(h, w1.astype(BF16), bias=b1, glu=True, name="conv_glu")
    z = dwconv_ln_silu(u, dw, dw_b, ln_g, ln_b, lay)
    return linear(z, w2.astype(BF16), bias=b2, resid=(x, mods, 2, lay), name="conv_out")


def rwkv_layer(x, mods, lay, norm_g, state, mu, wr, wk, wv, wo, w0, w1, w2, a0, a1, a2, g1, g2,
               k_k, k_a, r_k, lnx_g, lnx_b):
    t, d = x.shape
    xr, xw, xk, xv, xa, xg = rwkv_mix(x, norm_g, mods, mu, lay)
    bf = lambda w: w.astype(BF16)
    r = linear(xr, bf(wr), name="rwkv_r")
    k = linear(xk, bf(wk), name="rwkv_k")
    v = linear(xv, bf(wv), name="rwkv_v")
    gate = linear(linear(xg, bf(g1), act="sigmoid", out_dtype=BF16, name="rwkv_g1"), bf(g2), name="rwkv_g2")
    lora = LANES * pl.cdiv(w1.shape[-1], LANES)
    nc = _ctx_rows(lay)
    ng = d // (WKV_GROUP * RWKV_HEAD)
    s_zero = jnp.zeros((lay.n_ctx, ng, RWKV_HEAD, WKV_GROUP * RWKV_HEAD), F32)
    ys, bos, finals = [], [], []
    for dirn, rev in enumerate((False, True)):
        tw = linear(xw, bf(_pad_cols(w1[dirn], lora)), act="tanh", out_dtype=BF16, name="rwkv_w1")
        wl = linear(tw, bf(_pad_rows(w2[dirn], lora)), bias=w0[dirn], name="rwkv_w2")
        ta = linear(xa, bf(_pad_cols(a1[dirn], lora)), out_dtype=BF16, name="rwkv_a1")
        al = linear(ta, bf(_pad_rows(a2[dirn], lora)), bias=a0[dirn], name="rwkv_a2")
        scan = functools.partial(wkv_scan, r, k, v, wl, al, k_k, k_a, r_k.reshape(-1), reverse=rev)
        yc, bc, sf = scan(s_zero, row0=0, nbatch=lay.n_ctx, seq=lay.ctx_seq)
        yl, bl, _ = scan(_state_to_groups(state[:, dirn]), row0=nc, nbatch=lay.n_lat, seq=lay.lat_seq)
        ys.append(jnp.concatenate([yc, yl], axis=0))
        bos.append(jnp.concatenate([bc, bl], axis=0))
        finals.append(_groups_to_state(sf))
    o = rwkv_out(ys[0], ys[1], bos[0], bos[1], gate, lnx_g, lnx_b)
    x = linear(o, bf(wo), resid=(x, mods, 2, lay), name="rwkv_out_proj")
    return x, jnp.stack(finals, axis=1)


def peer_layer(x, mods, lay, norm_g, wq, keys, u_tab, v_tab):
    h_t = modulate(x, norm_g, mods, 3, lay, transpose=True)
    q_t = linear(wq.T.astype(BF16), h_t, name="peer_query")
    ea, e1, th = peer_route(q_t, keys.reshape(PEER_HEADS * 2, N_KEYS, -1))
    return peer_dense(h_t, u_tab.astype(BF16), v_tab.T.astype(BF16), ea, e1, th, x, mods, 5, lay)


def kernel(x_prompt, x_sample, cache_k, cache_v, state_wkv, c, c_ctx, norm1, norm2, ada_w, ada_b, attn_wqkv, attn_q_norm, attn_k_norm, attn_wo, conv_w1, conv_b1, conv_dw, conv_dw_b, conv_ln_g, conv_ln_b, conv_w2, conv_b2, rwkv_mu, rwkv_wr, rwkv_wk, rwkv_wv, rwkv_wo, rwkv_w0, rwkv_w1, rwkv_w2, rwkv_a0, rwkv_a1, rwkv_a2, rwkv_g1, rwkv_g2, rwkv_k_k, rwkv_k_a, rwkv_r_k, rwkv_lnx_g, rwkv_lnx_b, peer_wq, peer_keys, peer_u, peer_v, final_norm):
    n_ctx, ctx_seq, d = x_prompt.shape
    n_lat, lat_seq, _ = x_sample.shape
    lay = Layout(n_ctx, ctx_seq, n_lat, lat_seq)
    depth = norm1.shape[0]
    x = jnp.concatenate([x_prompt.reshape(n_ctx * ctx_seq, d), x_sample.reshape(n_lat * lat_seq, d)], axis=0)
    n_cond = 1 + n_lat
    cond = jnp.zeros((SUBLANES * pl.cdiv(n_cond, SUBLANES), d), F32).at[0].set(c_ctx).at[1:n_cond].set(c)
    cond = jax.nn.silu(cond).astype(BF16)
    new_k, new_v, new_s = [], [], []
    for i in range(depth):
        kind, j = i % 3, i // 3
        m = linear(cond, ada_w[i].astype(BF16), bias=ada_b[i], tn=1024, name="ada_mods")
        mods = m[:n_cond].reshape(n_cond * 6, 1, d)
        if kind == 0:
            x, nk, nv = attention_layer(x, mods, lay, norm1[i], attn_wqkv[j], attn_q_norm[j], attn_k_norm[j],
                                        attn_wo[j], cache_k[:, j], cache_v[:, j])
            new_k.append(nk)
            new_v.append(nv)
        elif kind == 1:
            x = conv_layer(x, mods, lay, norm1[i], conv_w1[j], conv_b1[j], conv_dw[j], conv_dw_b[j],
                           conv_ln_g[j], conv_ln_b[j], conv_w2[j], conv_b2[j])
        else:
            x, ns = rwkv_layer(x, mods, lay, norm1[i], state_wkv[:, j], rwkv_mu[j], rwkv_wr[j], rwkv_wk[j],
                               rwkv_wv[j], rwkv_wo[j], rwkv_w0[j], rwkv_w1[j], rwkv_w2[j], rwkv_a0[j],
                               rwkv_a1[j], rwkv_a2[j], rwkv_g1[j], rwkv_g2[j], rwkv_k_k[j], rwkv_k_a[j],
                               rwkv_r_k[j], rwkv_lnx_g[j], rwkv_lnx_b[j])
            new_s.append(ns)
        x = peer_layer(x, mods, lay, norm2[i], peer_wq[i], peer_keys[i], peer_u[i], peer_v[i])
    y = rmsnorm(x, final_norm)
    nc = n_ctx * ctx_seq
    return (y[:nc].reshape(n_ctx, ctx_seq, d), y[nc:].reshape(n_lat, lat_seq, d),
            jnp.stack(new_k, axis=1), jnp.stack(new_v, axis=1), jnp.stack(new_s, axis=1))
```

```python
import collections
import functools
import math

import jax
import jax.numpy as jnp
from jax import lax
from jax.experimental import pallas as pl
from jax.experimental.pallas import tpu as pltpu

F32 = jnp.float32
BF16 = jnp.bfloat16
HIGHEST = lax.Precision.HIGHEST

HEAD_DIM = 128
KV_GROUP = 4
ROPE_THETA = 10000.0
GRID_W = 64
CONV_WIDTH = 31
RWKV_HEAD = 64
GN_EPS = 64e-5
N_KEYS = 128
PEER_HEADS = 8
PEER_TOPK = 16
NORM_EPS = 1e-6
LN_EPS = 1e-5

LANES = 128
SUBLANES = 8
VMEM_LIMIT_BYTES = 56 * 1024 * 1024
MXU_ROWS = 512

WKV_CHUNK = 64
WKV_GROUP = 4

Layout = collections.namedtuple("Layout", "n_ctx ctx_seq n_lat lat_seq")


def _ctx_rows(lay):
    return lay.n_ctx * lay.ctx_seq


def _rows(lay):
    return _ctx_rows(lay) + lay.n_lat * lay.lat_seq


def _cond_of_tile(i, tm, lay):
    n_ctx_tiles = _ctx_rows(lay) // tm
    per = lay.lat_seq // tm
    return jnp.where(i < n_ctx_tiles, 0, 1 + (i - n_ctx_tiles) // per)


def _params(*sem):
    return pltpu.CompilerParams(dimension_semantics=sem, vmem_limit_bytes=VMEM_LIMIT_BYTES)


def _tile(n, pref):
    t = min(n, pref)
    while n % t:
        t //= 2
    assert t == n or t % LANES == 0, (n, pref)
    return t


def _mod_spec(which, tm, lay, d, ngrid):
    if ngrid == 1:
        return pl.BlockSpec((1, 1, d), lambda i: (_cond_of_tile(i, tm, lay) * 6 + which, 0, 0))
    return pl.BlockSpec((1, 1, d), lambda i, j: (_cond_of_tile(i, tm, lay) * 6 + which, 0, 0))


def _rms_mod(x, g, shift, scale):
    ms = jnp.mean(x * x, axis=-1, keepdims=True)
    return (x * lax.rsqrt(ms + NORM_EPS) * g) * (1.0 + scale) + shift


def _modulate_kernel(x_ref, g_ref, sh_ref, sc_ref, o_ref, *, transpose):
    h = _rms_mod(x_ref[...], g_ref[...], sh_ref[0], sc_ref[0])
    if transpose:
        o_ref[...] = h.T.astype(o_ref.dtype)
    else:
        o_ref[...] = h.astype(o_ref.dtype)


def modulate(x, g, mods, which_shift, lay, *, out_dtype=BF16, transpose=False, tm=256):
    t, d = x.shape
    tm = _tile(math.gcd(lay.ctx_seq * lay.n_ctx, lay.lat_seq), tm)
    out_shape = (d, t) if transpose else (t, d)
    out_spec = pl.BlockSpec((d, tm), lambda i: (0, i)) if transpose else pl.BlockSpec((tm, d), lambda i: (i, 0))
    return pl.pallas_call(
        functools.partial(_modulate_kernel, transpose=transpose),
        grid=(t // tm,),
        in_specs=[pl.BlockSpec((tm, d), lambda i: (i, 0)),
                  pl.BlockSpec((1, d), lambda i: (0, 0)),
                  _mod_spec(which_shift, tm, lay, d, 1),
                  _mod_spec(which_shift + 1, tm, lay, d, 1)],
        out_specs=out_spec,
        out_shape=jax.ShapeDtypeStruct(out_shape, out_dtype),
        compiler_params=_params("parallel"),
        name="modulate_t" if transpose else "modulate",
    )(x, g.reshape(1, d), mods, mods)


def _linear_kernel(*refs, has_bias, act, glu, resid):
    it = iter(refs)
    a_ref = next(it)
    w_ref = next(it)
    w2_ref = next(it) if glu else None
    b_ref = next(it) if has_bias else None
    b2_ref = next(it) if (glu and has_bias) else None
    x_ref = next(it) if resid else None
    gt_ref = next(it) if resid else None
    o_ref = next(it)
    a = a_ref[...].astype(BF16)
    acc = jnp.dot(a, w_ref[...], preferred_element_type=F32)
    if has_bias:
        acc = acc + b_ref[...]
    if glu:
        acc2 = jnp.dot(a, w2_ref[...], preferred_element_type=F32)
        if has_bias:
            acc2 = acc2 + b2_ref[...]
        acc = acc * jax.nn.sigmoid(acc2)
    if act == "sigmoid":
        acc = jax.nn.sigmoid(acc)
    elif act == "tanh":
        acc = jnp.tanh(acc)
    if resid:
        acc = x_ref[...] + gt_ref[0] * acc
    o_ref[...] = acc.astype(o_ref.dtype)


def linear(a, w, *, bias=None, act=None, glu=False, resid=None, out_dtype=F32, tm=1024, tn=512, name="linear"):
    m, k = a.shape
    n = w.shape[1] // 2 if glu else w.shape[1]
    if resid is not None:
        tm = _tile(math.gcd(_ctx_rows(resid[3]), resid[3].lat_seq), tm)
    tm = _tile(m, tm)
    tn = _tile(n, tn)
    nj = n // tn
    in_specs = [pl.BlockSpec((tm, k), lambda i, j: (i, 0)),
                pl.BlockSpec((k, tn), lambda i, j: (0, j))]
    args = [a, w]
    if glu:
        in_specs.append(pl.BlockSpec((k, tn), lambda i, j: (0, j + nj)))
        args.append(w)
    if bias is not None:
        b2d = bias.reshape(1, -1).astype(F32)
        in_specs.append(pl.BlockSpec((1, tn), lambda i, j: (0, j)))
        args.append(b2d)
        if glu:
            in_specs.append(pl.BlockSpec((1, tn), lambda i, j: (0, j + nj)))
            args.append(b2d)
    if resid is not None:
        x, mods, which, lay = resid
        in_specs.append(pl.BlockSpec((tm, tn), lambda i, j: (i, j)))
        args.append(x)
        in_specs.append(pl.BlockSpec((1, 1, tn), lambda i, j: (_cond_of_tile(i, tm, lay) * 6 + which, 0, j)))
        args.append(mods)
    return pl.pallas_call(
        functools.partial(_linear_kernel, has_bias=bias is not None, act=act, glu=glu, resid=resid is not None),
        grid=(m // tm, nj),
        in_specs=in_specs,
        out_specs=pl.BlockSpec((tm, tn), lambda i, j: (i, j)),
        out_shape=jax.ShapeDtypeStruct((m, n), out_dtype),
        compiler_params=_params("parallel", "parallel"),
        name=name,
    )(*args)


def rope_tables(lat_seq):
    rows = lat_seq // GRID_W
    pos_row = jnp.repeat(jnp.arange(rows, dtype=F32), GRID_W)
    pos_col = jnp.tile(jnp.arange(GRID_W, dtype=F32), rows)
    n_freq = HEAD_DIM // 4
    inv = ROPE_THETA ** (-jnp.arange(n_freq, dtype=F32) / n_freq)
    ang_r = pos_row[:, None] * inv
    ang_c = pos_col[:, None] * inv
    cos = jnp.concatenate([jnp.cos(ang_r), jnp.cos(ang_r), jnp.cos(ang_c), jnp.cos(ang_c)], axis=1)
    sin = jnp.concatenate([-jnp.sin(ang_r), jnp.sin(ang_r), -jnp.sin(ang_c), jnp.sin(ang_c)], axis=1)
    return cos, sin


def _head_rms(x, g):
    ms = jnp.mean(x * x, axis=-1, keepdims=True)
    return x * lax.rsqrt(ms + NORM_EPS) * g


def _rope(x, cos, sin):
    q = HEAD_DIM // 4
    lane = lax.broadcasted_iota(jnp.int32, x.shape, 1)
    first = (lane // q) % 2 == 0
    partner = jnp.where(first, pltpu.roll(x, HEAD_DIM - q, 1), pltpu.roll(x, q, 1))
    return x * cos + partner * sin


def _qkv_post_kernel(*refs, n_heads, n_kv, rope, emit_f32):
    it = iter(refs)
    qkv_ref = next(it)
    qn_ref = next(it)
    kn_ref = next(it)
    cos_ref = next(it) if rope else None
    sin_ref = next(it) if rope else None
    q_ref = next(it)
    k_ref = next(it)
    v_ref = next(it)
    kf_ref = next(it) if emit_f32 else None
    vf_ref = next(it) if emit_f32 else None
    for h in range(n_heads + n_kv):
        x = qkv_ref[:, h * HEAD_DIM:(h + 1) * HEAD_DIM]
        is_q = h < n_heads
        y = _head_rms(x, qn_ref[...] if is_q else kn_ref[...])
        if emit_f32 and not is_q:
            kf_ref[:, (h - n_heads) * HEAD_DIM:(h - n_heads + 1) * HEAD_DIM] = y
        if rope:
            y = _rope(y, cos_ref[...], sin_ref[...])
        if is_q:
            q_ref[:, h * HEAD_DIM:(h + 1) * HEAD_DIM] = y.astype(q_ref.dtype)
        else:
            k_ref[:, (h - n_heads) * HEAD_DIM:(h - n_heads + 1) * HEAD_DIM] = y.astype(k_ref.dtype)
    v = qkv_ref[:, (n_heads + n_kv) * HEAD_DIM:]
    v_ref[...] = v.astype(v_ref.dtype)
    if emit_f32:
        vf_ref[...] = v


def qkv_post(qkv, q_norm, k_norm, *, row0, nrows, seq, rope_tabs, emit_f32, n_heads, tm=256):
    n_kv = n_heads // KV_GROUP
    dq, dk = n_heads * HEAD_DIM, n_kv * HEAD_DIM
    tm = _tile(seq, tm)
    assert row0 % tm == 0
    r0 = row0 // tm
    rope = rope_tabs is not None
    in_specs = [pl.BlockSpec((tm, dq + 2 * dk), lambda i: (i + r0, 0)),
                pl.BlockSpec((1, HEAD_DIM), lambda i: (0, 0)),
                pl.BlockSpec((1, HEAD_DIM), lambda i: (0, 0))]
    args = [qkv, q_norm.reshape(1, HEAD_DIM), k_norm.reshape(1, HEAD_DIM)]
    if rope:
        per = seq // tm
        in_specs += [pl.BlockSpec((tm, HEAD_DIM), lambda i: (i % per, 0))] * 2
        args += list(rope_tabs)
    out_shape = [jax.ShapeDtypeStruct((nrows, dq), BF16), jax.ShapeDtypeStruct((nrows, dk), BF16),
                 jax.ShapeDtypeStruct((nrows, dk), BF16)]
    out_specs = [pl.BlockSpec((tm, dq), lambda i: (i, 0)), pl.BlockSpec((tm, dk), lambda i: (i, 0)),
                 pl.BlockSpec((tm, dk), lambda i: (i, 0))]
    if emit_f32:
        out_shape += [jax.ShapeDtypeStruct((nrows, dk), F32)] * 2
        out_specs += [pl.BlockSpec((tm, dk), lambda i: (i, 0))] * 2
    return pl.pallas_call(
        functools.partial(_qkv_post_kernel, n_heads=n_heads, n_kv=n_kv, rope=rope, emit_f32=emit_f32),
        grid=(nrows // tm,),
        in_specs=in_specs, out_specs=out_specs, out_shape=out_shape,
        compiler_params=_params("parallel"),
        name="qkv_post_rope" if rope else "qkv_post",
    )(*args)


def _attn_kernel(*refs, has_cache):
    it = iter(refs)
    q_ref = next(it)
    k_ref = next(it)
    v_ref = next(it)
    kc_ref = next(it) if has_cache else None
    vc_ref = next(it) if has_cache else None
    o_ref = next(it)
    c = HEAD_DIM ** -0.5 * math.log2(math.e)
    nt = (((1,), (1,)), ((), ()))
    k = k_ref[...]
    v = v_ref[...]
    for h in range(KV_GROUP):
        q = q_ref[:, h * HEAD_DIM:(h + 1) * HEAD_DIM]
        s = lax.dot_general(q, k, nt, preferred_element_type=F32)
        m = jnp.max(s, axis=-1, keepdims=True)
        if has_cache:
            s2 = lax.dot_general(q, kc_ref[0], nt, preferred_element_type=F32)
            m = jnp.maximum(m, jnp.max(s2, axis=-1, keepdims=True))
        p = jnp.exp2((s - m) * c)
        l = jnp.sum(p, axis=-1, keepdims=True)
        o = jnp.dot(p.astype(BF16), v, preferred_element_type=F32)
        if has_cache:
            p2 = jnp.exp2((s2 - m) * c)
            l = l + jnp.sum(p2, axis=-1, keepdims=True)
            o = o + jnp.dot(p2.astype(BF16), vc_ref[0], preferred_element_type=F32)
        o_ref[:, h * HEAD_DIM:(h + 1) * HEAD_DIM] = (o / l).astype(o_ref.dtype)


def attention(q, k, v, cache_k, cache_v, *, nbatch, seq, tq=256):
    n_kv = k.shape[1] // HEAD_DIM
    tq = _tile(seq, tq)
    nq = seq // tq
    gw = KV_GROUP * HEAD_DIM
    has_cache = cache_k is not None
    in_specs = [pl.BlockSpec((tq, gw), lambda b, g, i: (b * nq + i, g)),
                pl.BlockSpec((seq, HEAD_DIM), lambda b, g, i: (b, g)),
                pl.BlockSpec((seq, HEAD_DIM), lambda b, g, i: (b, g))]
    args = [q, k, v]
    if has_cache:
        past = cache_k.shape[1]
        in_specs += [pl.BlockSpec((1, past, HEAD_DIM), lambda b, g, i: (b, 0, g))] * 2
        args += [cache_k, cache_v]
    return pl.pallas_call(
        functools.partial(_attn_kernel, has_cache=has_cache),
        grid=(nbatch, n_kv, nq),
        in_specs=in_specs,
        out_specs=pl.BlockSpec((tq, gw), lambda b, g, i: (b * nq + i, g)),
        out_shape=jax.ShapeDtypeStruct(q.shape, BF16),
        compiler_params=_params("parallel", "parallel", "parallel"),
        name="attention_cache" if has_cache else "attention",
    )(*args)


def _seq_edges(i, tm, lay):
    n_ctx_tiles = _ctx_rows(lay) // tm
    per_c = lay.ctx_seq // tm
    per_l = lay.lat_seq // tm
    il = i - n_ctx_tiles
    first = jnp.where(i < n_ctx_tiles, i % per_c == 0, il % per_l == 0)
    last = jnp.where(i < n_ctx_tiles, i % per_c == per_c - 1, il % per_l == per_l - 1)
    return first, last


def _halo_specs(tm, halo, d, nblocks_total):
    r = tm // halo
    prev = pl.BlockSpec((halo, d), lambda i: (jnp.maximum(i * r - 1, 0), 0))
    nxt = pl.BlockSpec((halo, d), lambda i: (jnp.minimum((i + 1) * r, nblocks_total - 1), 0))
    return prev, nxt


CONV_HALO = 16


def _dwconv_kernel(u_ref, up_ref, un_ref, dw_ref, dwb_ref, g_ref, b_ref, o_ref, ext_ref, acc_ref, *, tm, lay, cb):
    i = pl.program_id(0)
    first, last = _seq_edges(i, tm, lay)
    d = u_ref.shape[1]
    ext_ref[CONV_HALO:CONV_HALO + tm, :] = u_ref[...]
    ext_ref[0:CONV_HALO, :] = jnp.where(first, 0.0, up_ref[...])
    ext_ref[CONV_HALO + tm:, :] = jnp.where(last, 0.0, un_ref[...])
    pad = CONV_WIDTH // 2

    def col_block(c, carry):
        cs = pl.multiple_of(c * cb, cb)
        w = ext_ref[:, pl.ds(cs, cb)]
        taps = dw_ref[:, pl.ds(cs, cb)]
        acc = jnp.zeros((tm, cb), F32)
        for j in range(CONV_WIDTH):
            o = CONV_HALO - pad + j
            acc = acc + w[o:o + tm] * taps[j:j + 1]
        acc_ref[:, pl.ds(cs, cb)] = acc
        return carry

    lax.fori_loop(0, d // cb, col_block, 0)
    y = acc_ref[...] + dwb_ref[...]
    mu = jnp.mean(y, axis=-1, keepdims=True)
    yc = y - mu
    var = jnp.mean(yc * yc, axis=-1, keepdims=True)
    z = yc * lax.rsqrt(var + LN_EPS) * g_ref[...] + b_ref[...]
    o_ref[...] = (z * jax.nn.sigmoid(z)).astype(o_ref.dtype)


def dwconv_ln_silu(u, dw, dw_b, ln_g, ln_b, lay, *, tm=128, cb=256):
    t, d = u.shape
    tm = _tile(math.gcd(lay.ctx_seq, lay.lat_seq), tm)
    cb = _tile(d, cb)
    prev, nxt = _halo_specs(tm, CONV_HALO, d, t // CONV_HALO)
    dwp = jnp.zeros((32, d), F32).at[:CONV_WIDTH].set(dw)
    row = lambda a: a.reshape(1, d)
    const = lambda r: pl.BlockSpec((r, d), lambda i: (0, 0))
    return pl.pallas_call(
        functools.partial(_dwconv_kernel, tm=tm, lay=lay, cb=cb),
        grid=(t // tm,),
        in_specs=[pl.BlockSpec((tm, d), lambda i: (i, 0)), prev, nxt,
                  const(32), const(1), const(1), const(1)],
        out_specs=pl.BlockSpec((tm, d), lambda i: (i, 0)),
        out_shape=jax.ShapeDtypeStruct((t, d), BF16),
        scratch_shapes=[pltpu.VMEM((tm + 2 * CONV_HALO, d), F32), pltpu.VMEM((tm, d), F32)],
        compiler_params=_params("parallel"),
        name="dwconv_ln_silu",
    )(u, u, u, dwp, row(dw_b), row(ln_g), row(ln_b))


MIX_HALO = 8


def _rwkv_mix_kernel(x_ref, xp_ref, xn_ref, g_ref, sh_ref, sc_ref, mu_ref, *o_refs, tm, lay):
    i = pl.program_id(0)
    first, last = _seq_edges(i, tm, lay)
    g, sh, sc = g_ref[...], sh_ref[0], sc_ref[0]
    h = _rms_mod(x_ref[...], g, sh, sc)
    hp = _rms_mod(xp_ref[MIX_HALO - 1:MIX_HALO, :], g, sh, sc)
    hn = _rms_mod(xn_ref[0:1, :], g, sh, sc)
    hp = jnp.where(first, 0.0, hp)
    hn = jnp.where(last, 0.0, hn)
    row = lax.broadcasted_iota(jnp.int32, h.shape, 0)
    h_prev = jnp.where(row == 0, hp, pltpu.roll(h, 1, 0))
    h_next = jnp.where(row == tm - 1, hn, pltpu.roll(h, tm - 1, 0))
    xx = 0.5 * (h_prev + h_next) - h
    for n, o_ref in enumerate(o_refs):
        o_ref[...] = (h + xx * mu_ref[n:n + 1, :]).astype(o_ref.dtype)


def rwkv_mix(x, g, mods, mu, lay, *, tm=256):
    t, d = x.shape
    tm = _tile(math.gcd(lay.ctx_seq, lay.lat_seq), tm)
    prev, nxt = _halo_specs(tm, MIX_HALO, d, t // MIX_HALO)
    mup = jnp.zeros((8, d), F32).at[:6].set(mu)
    return pl.pallas_call(
        functools.partial(_rwkv_mix_kernel, tm=tm, lay=lay),
        grid=(t // tm,),
        in_specs=[pl.BlockSpec((tm, d), lambda i: (i, 0)), prev, nxt,
                  pl.BlockSpec((1, d), lambda i: (0, 0)),
                  _mod_spec(0, tm, lay, d, 1), _mod_spec(1, tm, lay, d, 1),
                  pl.BlockSpec((8, d), lambda i: (0, 0))],
        out_specs=[pl.BlockSpec((tm, d), lambda i: (i, 0))] * 6,
        out_shape=[jax.ShapeDtypeStruct((t, d), BF16)] * 6,
        compiler_params=_params("parallel"),
        name="rwkv_mix",
    )(x, x, x, g.reshape(1, d), mods, mods, mup)


def _split_dot(x, w_bf16, passes, *, w_left=False):
    acc = None
    rem = x
    for _ in range(passes):
        hi = rem.astype(BF16)
        part = (jnp.dot(w_bf16, hi, preferred_element_type=F32) if w_left
                else jnp.dot(hi, w_bf16, preferred_element_type=F32))
        acc = part if acc is None else acc + part
        rem = rem - hi.astype(F32)
    return acc


def _wkv_groups(r, k, v, wl, al, kkp, ka, rk, s_old, *, reverse, masks):
    C = WKV_CHUNK
    N = RWKV_HEAD
    bd_f32, bd_bf16, tri, strict, incl, eye, off_masks = masks
    each = lambda f, *xs: [f(*a) for a in zip(*xs)]

    def bd(x):
        return jnp.concatenate([x.astype(BF16)] * WKV_GROUP, axis=0) * bd_bf16

    nt = (((1,), (1,)), ((), ()))
    dot_nt = lambda x, y: lax.dot_general(x.astype(BF16), y, nt, preferred_element_type=F32)
    dot_nn = lambda x, y: jnp.dot(x.astype(BF16), y, preferred_element_type=F32)

    iclr = each(jax.nn.sigmoid, al)
    lw = each(lambda w: -jnp.exp(-(jnp.maximum(-w, 0.0) + jnp.log1p(jnp.exp(-jnp.abs(w)))) - 0.5), wl)
    kx = each(lambda k_, p: k_ * p, k, kkp)
    kn = each(lambda x: _split_dot(x * x, bd_bf16, 2), kx)
    kk = each(lambda x, n: x / jnp.maximum(jnp.sqrt(n), 1e-12), kx, kn)
    kd = each(lambda k_, i, p: k_ * (1.0 + (i - 1.0) * p), k, iclr, ka)
    bsum = each(lambda r_, d_, p: _split_dot(r_ * d_ * p, bd_bf16, 2), r, kd, rk)
    bonus = each(lambda s, v_: s * v_, bsum, v)
    cum = each(lambda l: _split_dot(l, tri, 3, w_left=True), lw)
    g_in = each(jnp.exp, cum)
    g_inv = each(lambda c_: jnp.exp(-c_), cum)
    at = each(lambda kk_, c_, l: (-kk_ * jnp.exp(c_ - l)).astype(BF16), kk, cum, lw)
    rt = each(lambda r_, g: (r_ * g).astype(BF16), r, g_in)
    bt = each(lambda kk_, i, g: (kk_ * i * g).astype(BF16), kk, iclr, g_inv)
    kt = each(lambda d_, g: (d_ * g).astype(BF16), kd, g_inv)

    ar = each(lambda a_, r_: jnp.concatenate([a_, r_], axis=0), at, rt)
    p_b = each(lambda x, y: dot_nt(x, bd(y)), ar, bt)
    p_k = each(lambda x, y: dot_nt(x, bd(y)), ar, kt)
    p_s = each(lambda x, y: dot_nt(x, bd(y)), ar, s_old)
    bdv = each(bd, v)
    lp = each(lambda p: jnp.where(strict, p[:C], 0.0), p_b)
    rhs = each(lambda ps, pk, w: ps[:C] + dot_nn(jnp.where(strict, pk[:C], 0.0), w), p_s, p_k, bdv)
    tinv = each(lambda l: eye + jnp.where(off_masks[0], l, 0.0), lp)
    for off in off_masks[1:]:
        x = each(lambda t, l: dot_nn(t, bd(jnp.where(off, l, 0.0))), tinv, lp)
        tinv = each(lambda t, x_: t + dot_nn(x_, bd(t)), tinv, x)
    u = each(lambda t, x: dot_nn(t, bd(x)), tinv, rhs)
    y = each(lambda ps, pb, pk, u_, w: ps[C:] + dot_nn(jnp.where(incl, pb[C:], 0.0), bd(u_))
             + dot_nn(jnp.where(incl, pk[C:], 0.0), w), p_s, p_b, p_k, u, bdv)

    def new_state(u_, v_, b_, k_, s, g):
        uv = jnp.concatenate([u_, v_], axis=0)
        bk = jnp.concatenate([b_, k_], axis=0)
        f = dot_nn(uv.T, bk) * bd_f32
        zsum = f[0:N]
        for h in range(1, WKV_GROUP):
            zsum = zsum + f[h * N:(h + 1) * N]
        return (s + zsum) * (g[0:1] if reverse else g[C - 1:C])

    return y, bonus, each(new_state, u, v, bt, kt, s_old, g_in)


def _wkv_kernel(r_ref, k_ref, v_ref, wl_ref, al_ref, kkp_ref, ka_ref, rk_ref, s0_ref,
                y_ref, bo_ref, sf_ref, s_scr, *, reverse, nc, nsub):
    C = WKV_CHUNK
    N = RWKV_HEAD
    W = WKV_GROUP * N
    c = pl.program_id(2)

    @pl.when(c == 0)
    def _():
        s_scr[...] = s0_ref[0, 0]

    lane_i = lax.broadcasted_iota(jnp.int32, (W, W), 0)
    lane_j = lax.broadcasted_iota(jnp.int32, (W, W), 1)
    bd_f32 = ((lane_i // N) == (lane_j // N)).astype(F32)
    tt = lax.broadcasted_iota(jnp.int32, (C, C), 0)
    ss = lax.broadcasted_iota(jnp.int32, (C, C), 1)
    tri = ((ss >= tt) if reverse else (ss <= tt)).astype(BF16)
    t2 = lax.broadcasted_iota(jnp.int32, (C, W), 0)
    s2 = lax.broadcasted_iota(jnp.int32, (C, W), 1) % C
    strict = (s2 > t2) if reverse else (s2 < t2)
    incl = (s2 >= t2) if reverse else (s2 <= t2)
    eye = (s2 == t2).astype(F32)
    off_masks = []
    b = 1
    while b < C:
        lo, hi = (s2, t2) if reverse else (t2, s2)
        off_masks.append(((t2 // (2 * b)) == (s2 // (2 * b))) & ((lo % (2 * b)) >= b) & ((hi % (2 * b)) < b))
        b *= 2
    masks = (bd_f32, bd_f32.astype(BF16), tri, strict, incl, eye, off_masks)

    sls = [slice(gi * W, (gi + 1) * W) for gi in range(nsub)]
    groups = lambda ref: [ref[:, sl] for sl in sls]
    y, bonus, s_new = _wkv_groups(*map(groups, (r_ref, k_ref, v_ref, wl_ref, al_ref, kkp_ref, ka_ref, rk_ref, s_scr)),
                                  reverse=reverse, masks=masks)
    for gi, sl in enumerate(sls):
        y_ref[:, sl] = y[gi]
        bo_ref[:, sl] = bonus[gi]
        s_scr[:, sl] = s_new[gi]

    @pl.when(c == nc - 1)
    def _():
        sf_ref[0, 0] = s_scr[...]


def wkv_scan(r, k, v, wl, al, k_k, k_a, r_k, s0, *, row0, nbatch, seq, reverse, nsub):
    d = r.shape[1]
    C, W = WKV_CHUNK, WKV_GROUP * RWKV_HEAD * nsub
    assert seq % C == 0 and row0 % C == 0 and d % W == 0
    nc, ng, rb0 = seq // C, d // W, row0 // C

    def chunk(c):
        return (nc - 1 - c) if reverse else c

    tok_in = pl.BlockSpec((C, W), lambda b, g, c: (rb0 + b * nc + chunk(c), g))
    tok_out = pl.BlockSpec((C, W), lambda b, g, c: (b * nc + chunk(c), g))
    par = pl.BlockSpec((1, W), lambda b, g, c: (0, g))
    st = pl.BlockSpec((1, 1, RWKV_HEAD, W), lambda b, g, c: (b, g, 0, 0))
    row = lambda x: x.reshape(1, d)
    return pl.pallas_call(
        functools.partial(_wkv_kernel, reverse=reverse, nc=nc, nsub=nsub),
        grid=(nbatch, ng, nc),
        in_specs=[tok_in] * 5 + [par] * 3 + [st],
        out_specs=[tok_out, tok_out, st],
        out_shape=[jax.ShapeDtypeStruct((nbatch * seq, d), F32), jax.ShapeDtypeStruct((nbatch * seq, d), F32),
                   jax.ShapeDtypeStruct(s0.shape, F32)],
        scratch_shapes=[pltpu.VMEM((RWKV_HEAD, W), F32)],
        compiler_params=_params("parallel", "parallel", "arbitrary"),
        name="wkv_scan_rev" if reverse else "wkv_scan_fwd",
    )(r, k, v, wl, al, row(k_k), row(k_a), row(r_k), s0)


def _rwkv_out_kernel(y0_ref, y1_ref, b0_ref, b1_ref, gate_ref, g_ref, b_ref, o_ref):
    d = y0_ref.shape[1]
    li = lax.broadcasted_iota(jnp.int32, (LANES, LANES), 0)
    lj = lax.broadcasted_iota(jnp.int32, (LANES, LANES), 1)
    avg = ((li // RWKV_HEAD) == (lj // RWKV_HEAD)).astype(F32) * (1.0 / RWKV_HEAD)
    for c in range(d // LANES):
        sl = slice(c * LANES, (c + 1) * LANES)
        y = y0_ref[:, sl] + y1_ref[:, sl]
        mu = jnp.dot(y, avg, precision=HIGHEST, preferred_element_type=F32)
        yc = y - mu
        var = jnp.dot(yc * yc, avg, precision=HIGHEST, preferred_element_type=F32)
        yn = yc * lax.rsqrt(var + GN_EPS) * g_ref[:, sl] + b_ref[:, sl]
        o_ref[:, sl] = ((yn + b0_ref[:, sl] + b1_ref[:, sl]) * gate_ref[:, sl]).astype(o_ref.dtype)


def rwkv_out(y0, y1, b0, b1, gate, lnx_g, lnx_b, *, tm=256):
    t, d = y0.shape
    tm = _tile(t, tm)
    tok = pl.BlockSpec((tm, d), lambda i: (i, 0))
    par = pl.BlockSpec((1, d), lambda i: (0, 0))
    return pl.pallas_call(
        _rwkv_out_kernel,
        grid=(t // tm,),
        in_specs=[tok] * 5 + [par] * 2,
        out_specs=tok,
        out_shape=jax.ShapeDtypeStruct((t, d), BF16),
        compiler_params=_params("parallel"),
        name="rwkv_out",
    )(y0, y1, b0, b1, gate, lnx_g.reshape(1, d), lnx_b.reshape(1, d))


def _extract_desc(x, n):
    vals = []
    cur = x
    for _ in range(n):
        mx = jnp.max(cur, axis=0, keepdims=True)
        vals.append(jnp.maximum(mx, 0.0))
        cur = jnp.where(cur == mx, -1.0, cur)
    return vals


def _peer_route_kernel(q_ref, keys_ref, ea_ref, e1_ref, th_ref):
    kp = PEER_TOPK
    half = kp // 2
    ths = []
    for h in range(PEER_HEADS):
        es = []
        tops = []
        for p in range(2):
            idx = h * 2 + p
            qhp = q_ref[idx * N_KEYS:(idx + 1) * N_KEYS, :]
            s = jnp.dot(keys_ref[idx], qhp, precision=HIGHEST, preferred_element_type=F32)
            e = jnp.exp(s - jnp.max(s, axis=0, keepdims=True))
            es.append(e)
            tops.append(jnp.concatenate(_extract_desc(e, kp), axis=0))
        v0, v1 = tops

        def cands(a0):
            blk = [a0[a:a + 1] * v1[:half] for a in range(half)]
            blk.append(a0[0:1] * v1[half:])
            blk.append(a0[half:] * v1[0:1])
            return jnp.concatenate(blk, axis=0)

        cand = cands(v0)
        best = _extract_desc(cand, kp)
        zsum = best[0]
        for b in best[1:]:
            zsum = zsum + b
        inv_z = 0.5 / zsum
        cand_n = cands(v0 * inv_z)
        th = jnp.min(jnp.where(cand >= best[kp - 1], cand_n, jnp.inf), axis=0, keepdims=True)
        ths.append(th)
        ea_ref[h * N_KEYS:(h + 1) * N_KEYS, :] = es[0] * inv_z
        e1_ref[h * N_KEYS:(h + 1) * N_KEYS, :] = es[1]
    th_ref[...] = jnp.concatenate(ths, axis=0)


def peer_route(q_t, keys, *, tn=256):
    rows, t = q_t.shape
    tn = _tile(t, tn)
    hk = PEER_HEADS * N_KEYS
    return pl.pallas_call(
        _peer_route_kernel,
        grid=(t // tn,),
        in_specs=[pl.BlockSpec((rows, tn), lambda i: (0, i)),
                  pl.BlockSpec(keys.shape, lambda i: (0, 0, 0))],
        out_specs=[pl.BlockSpec((hk, tn), lambda i: (0, i)), pl.BlockSpec((hk, tn), lambda i: (0, i)),
                   pl.BlockSpec((PEER_HEADS, tn), lambda i: (0, i))],
        out_shape=[jax.ShapeDtypeStruct((hk, t), F32), jax.ShapeDtypeStruct((hk, t), F32),
                   jax.ShapeDtypeStruct((PEER_HEADS, t), F32)],
        compiler_params=_params("parallel"),
        name="peer_route",
    )(q_t, keys)


def _peer_dense_kernel(h_ref, u_ref, vt_ref, ea_ref, e1_ref, th_ref, x_ref, gt_ref, o_ref, acc_ref, w_ref, *, ni):
    e = pl.program_id(1)

    @pl.when(e == 0)
    def _():
        acc_ref[...] = jnp.zeros_like(acc_ref)

    hv = h_ref[...]
    a = jnp.concatenate([jnp.dot(u_ref[r0:r0 + MXU_ROWS, :], hv, preferred_element_type=F32)
                         for r0 in range(0, u_ref.shape[0], MXU_ROWS)], axis=0)
    tn = h_ref.shape[1]
    lw = min(tn, 2 * LANES)
    for l0 in range(0, tn, lw):
        ls = slice(l0, l0 + lw)
        ths = [th_ref[h:h + 1, ls] for h in range(PEER_HEADS)]
        for i8 in range(ni // SUBLANES):
            base = pl.multiple_of(e * ni + i8 * SUBLANES, SUBLANES)
            tiles = [ea_ref[pl.ds(h * N_KEYS + base, SUBLANES), ls] for h in range(PEER_HEADS)]
            for ii in range(SUBLANES):
                rows = [t[ii:ii + 1] for t in tiles]
                for rb in range(N_KEYS // SUBLANES):
                    w = None
                    for h in range(PEER_HEADS):
                        r0 = h * N_KEYS + rb * SUBLANES
                        p = rows[h] * e1_ref[r0:r0 + SUBLANES, ls]
                        sel = jnp.where(p >= ths[h], p, 0.0)
                        w = sel if w is None else w + sel
                    w0 = (i8 * SUBLANES + ii) * N_KEYS + rb * SUBLANES
                    w_ref[w0:w0 + SUBLANES, ls] = w
    act = a * (1.0 + lax.erf(a * (2.0 ** -0.5)))
    gw = (w_ref[...] * act).astype(BF16)
    for r0 in range(0, acc_ref.shape[0], MXU_ROWS):
        rs = slice(r0, r0 + MXU_ROWS)
        acc_ref[rs, :] += jnp.dot(vt_ref[rs, :], gw, preferred_element_type=F32)

    @pl.when(e == pl.num_programs(1) - 1)
    def _():
        o_ref[...] = x_ref[...] + gt_ref[0] * acc_ref[...].T


def peer_dense(h_t, u_tab, vt_tab, ea, e1, th, x, mods, which_gate, lay, *, tn=512, te=1024):
    d, t = h_t.shape
    n_exp = u_tab.shape[0]
    tn = _tile(math.gcd(_ctx_rows(lay), lay.lat_seq), tn)
    te = _tile(n_exp, te)
    ni = te // N_KEYS
    assert ni % SUBLANES == 0
    hk = PEER_HEADS * N_KEYS
    return pl.pallas_call(
        functools.partial(_peer_dense_kernel, ni=ni),
        grid=(t // tn, n_exp // te),
        in_specs=[pl.BlockSpec((d, tn), lambda i, e: (0, i)),
                  pl.BlockSpec((te, d), lambda i, e: (e, 0)),
                  pl.BlockSpec((d, te), lambda i, e: (0, e)),
                  pl.BlockSpec((hk, tn), lambda i, e: (0, i)),
                  pl.BlockSpec((hk, tn), lambda i, e: (0, i)),
                  pl.BlockSpec((PEER_HEADS, tn), lambda i, e: (0, i)),
                  pl.BlockSpec((tn, d), lambda i, e: (i, 0)),
                  pl.BlockSpec((1, 1, d), lambda i, e: (_cond_of_tile(i, tn, lay) * 6 + which_gate, 0, 0))],
        out_specs=pl.BlockSpec((tn, d), lambda i, e: (i, 0)),
        out_shape=jax.ShapeDtypeStruct((t, d), F32),
        scratch_shapes=[pltpu.VMEM((d, tn), F32), pltpu.VMEM((te, tn), F32)],
        compiler_params=_params("parallel", "arbitrary"),
        name="peer_dense",
    )(h_t, u_tab, vt_tab, ea, e1, th, x, mods)


def _rmsnorm_kernel(x_ref, g_ref, o_ref):
    x = x_ref[...]
    ms = jnp.mean(x * x, axis=-1, keepdims=True)
    o_ref[...] = x * lax.rsqrt(ms + NORM_EPS) * g_ref[...]


def rmsnorm(x, g, *, tm=512):
    t, d = x.shape
    tm = _tile(t, tm)
    return pl.pallas_call(
        _rmsnorm_kernel,
        grid=(t // tm,),
        in_specs=[pl.BlockSpec((tm, d), lambda i: (i, 0)), pl.BlockSpec((1, d), lambda i: (0, 0))],
        out_specs=pl.BlockSpec((tm, d), lambda i: (i, 0)),
        out_shape=jax.ShapeDtypeStruct((t, d), F32),
        compiler_params=_params("parallel"),
        name="final_rmsnorm",
    )(x, g.reshape(1, d))


def _pad_cols(w, n):
    return jnp.zeros(w.shape[:-1] + (n,), w.dtype).at[..., :w.shape[-1]].set(w)


def _pad_rows(w, n):
    return jnp.zeros((n,) + w.shape[1:], w.dtype).at[:w.shape[0]].set(w)


def _state_to_groups(s, hg):
    b, h, n, _ = s.shape
    return s.reshape(b, h // hg, hg, n, n).transpose(0, 1, 3, 2, 4).reshape(b, h // hg, n, hg * n)


def _groups_to_state(s, hg):
    b, g, n, _ = s.shape
    return s.reshape(b, g, n, hg, n).transpose(0, 1, 3, 2, 4).reshape(b, g * hg, n, n)


def attention_layer(x, mods, lay, norm_g, wqkv, q_norm, k_norm, wo, cache_k, cache_v):
    t, d = x.shape
    n_heads = d // HEAD_DIM
    n_kv = n_heads // KV_GROUP
    h = modulate(x, norm_g, mods, 0, lay)
    qkv = linear(h, wqkv.astype(BF16), name="attn_qkv")
    nc = _ctx_rows(lay)
    qc, kc, vc, kcf, vcf = qkv_post(qkv, q_norm, k_norm, row0=0, nrows=nc, seq=lay.ctx_seq,
                                    rope_tabs=None, emit_f32=True, n_heads=n_heads)
    ql, kl, vl = qkv_post(qkv, q_norm, k_norm, row0=nc, nrows=t - nc, seq=lay.lat_seq,
                          rope_tabs=rope_tables(lay.lat_seq), emit_f32=False, n_heads=n_heads)
    oc = attention(qc, kc, vc, None, None, nbatch=lay.n_ctx, seq=lay.ctx_seq)
    past = cache_k.shape[1]
    ck = cache_k.reshape(lay.n_lat, past, n_kv * HEAD_DIM).astype(BF16)
    cv = cache_v.reshape(lay.n_lat, past, n_kv * HEAD_DIM).astype(BF16)
    ol = attention(ql, kl, vl, ck, cv, nbatch=lay.n_lat, seq=lay.lat_seq)
    o = jnp.concatenate([oc, ol], axis=0)
    x = linear(o, wo.astype(BF16), resid=(x, mods, 2, lay), name="attn_out")
    new_k = kcf.reshape(lay.n_ctx, lay.ctx_seq, n_kv, HEAD_DIM)
    new_v = vcf.reshape(lay.n_ctx, lay.ctx_seq, n_kv, HEAD_DIM)
    return x, new_k, new_v


def conv_layer(x, mods, lay, norm_g, w1, b1, dw, dw_b, ln_g, ln_b, w2, b2):
    h = modulate(x, norm_g, mods, 0, lay)
    u = linear(h, w1.astype(BF16), bias=b1, glu=True, name="conv_glu")
    z = dwconv_ln_silu(u, dw, dw_b, ln_g, ln_b, lay)
    return linear(z, w2.astype(BF16), bias=b2, resid=(x, mods, 2, lay), name="conv_out")


def rwkv_layer(x, mods, lay, norm_g, state, mu, wr, wk, wv, wo, w0, w1, w2, a0, a1, a2, g1, g2,
               k_k, k_a, r_k, lnx_g, lnx_b):
    t, d = x.shape
    xr, xw, xk, xv, xa, xg = rwkv_mix(x, norm_g, mods, mu, lay)
    bf = lambda w: w.astype(BF16)
    r = linear(xr, bf(wr), name="rwkv_r")
    k = linear(xk, bf(wk), name="rwkv_k")
    v = linear(xv, bf(wv), name="rwkv_v")
    gate = linear(linear(xg, bf(g1), act="sigmoid", out_dtype=BF16, name="rwkv_g1"), bf(g2), name="rwkv_g2")
    lora = LANES * pl.cdiv(w1.shape[-1], LANES)
    nc = _ctx_rows(lay)
    nsub = math.gcd(d // (WKV_GROUP * RWKV_HEAD), 4)
    hg = WKV_GROUP * nsub
    s_zero = jnp.zeros((lay.n_ctx, d // (hg * RWKV_HEAD), RWKV_HEAD, hg * RWKV_HEAD), F32)
    ys, bos, finals = [], [], []
    for dirn, rev in enumerate((False, True)):
        tw = linear(xw, bf(_pad_cols(w1[dirn], lora)), act="tanh", out_dtype=BF16, name="rwkv_w1")
        wl = linear(tw, bf(_pad_rows(w2[dirn], lora)), bias=w0[dirn], name="rwkv_w2")
        ta = linear(xa, bf(_pad_cols(a1[dirn], lora)), out_dtype=BF16, name="rwkv_a1")
        al = linear(ta, bf(_pad_rows(a2[dirn], lora)), bias=a0[dirn], name="rwkv_a2")
        scan = functools.partial(wkv_scan, r, k, v, wl, al, k_k, k_a, r_k.reshape(-1), reverse=rev, nsub=nsub)
        yc, bc, sf = scan(s_zero, row0=0, nbatch=lay.n_ctx, seq=lay.ctx_seq)
        yl, bl, _ = scan(_state_to_groups(state[:, dirn], hg), row0=nc, nbatch=lay.n_lat, seq=lay.lat_seq)
        ys.append(jnp.concatenate([yc, yl], axis=0))
        bos.append(jnp.concatenate([bc, bl], axis=0))
        finals.append(_groups_to_state(sf, hg))
    o = rwkv_out(ys[0], ys[1], bos[0], bos[1], gate, lnx_g, lnx_b)
    x = linear(o, bf(wo), resid=(x, mods, 2, lay), name="rwkv_out_proj")
    return x, jnp.stack(finals, axis=1)


def peer_layer(x, mods, lay, norm_g, wq, keys, u_tab, v_tab):
    h_t = modulate(x, norm_g, mods, 3, lay, transpose=True)
    q_t = linear(wq.T.astype(BF16), h_t, name="peer_query")
    ea, e1, th = peer_route(q_t, keys.reshape(PEER_HEADS * 2, N_KEYS, -1))
    return peer_dense(h_t, u_tab.astype(BF16), v_tab.T.astype(BF16), ea, e1, th, x, mods, 5, lay)


def kernel(x_prompt, x_sample, cache_k, cache_v, state_wkv, c, c_ctx, norm1, norm2, ada_w, ada_b, attn_wqkv, attn_q_norm, attn_k_norm, attn_wo, conv_w1, conv_b1, conv_dw, conv_dw_b, conv_ln_g, conv_ln_b, conv_w2, conv_b2, rwkv_mu, rwkv_wr, rwkv_wk, rwkv_wv, rwkv_wo, rwkv_w0, rwkv_w1, rwkv_w2, rwkv_a0, rwkv_a1, rwkv_a2, rwkv_g1, rwkv_g2, rwkv_k_k, rwkv_k_a, rwkv_r_k, rwkv_lnx_g, rwkv_lnx_b, peer_wq, peer_keys, peer_u, peer_v, final_norm):
    n_ctx, ctx_seq, d = x_prompt.shape
    n_lat, lat_seq, _ = x_sample.shape
    lay = Layout(n_ctx, ctx_seq, n_lat, lat_seq)
    depth = norm1.shape[0]
    x = jnp.concatenate([x_prompt.reshape(n_ctx * ctx_seq, d), x_sample.reshape(n_lat * lat_seq, d)], axis=0)
    n_cond = 1 + n_lat
    cond = jnp.zeros((SUBLANES * pl.cdiv(n_cond, SUBLANES), d), F32).at[0].set(c_ctx).at[1:n_cond].set(c)
    cond = jax.nn.silu(cond).astype(BF16)
    new_k, new_v, new_s = [], [], []
    for i in range(depth):
        kind, j = i % 3, i // 3
        m = linear(cond, ada_w[i].astype(BF16), bias=ada_b[i], tn=1024, name="ada_mods")
        mods = m[:n_cond].reshape(n_cond * 6, 1, d)
        if kind == 0:
            x, nk, nv = attention_layer(x, mods, lay, norm1[i], attn_wqkv[j], attn_q_norm[j], attn_k_norm[j],
                                        attn_wo[j], cache_k[:, j], cache_v[:, j])
            new_k.append(nk)
            new_v.append(nv)
        elif kind == 1:
            x = conv_layer(x, mods, lay, norm1[i], conv_w1[j], conv_b1[j], conv_dw[j], conv_dw_b[j],
                           conv_ln_g[j], conv_ln_b[j], conv_w2[j], conv_b2[j])
        else:
            x, ns = rwkv_layer(x, mods, lay, norm1[i], state_wkv[:, j], rwkv_mu[j], rwkv_wr[j], rwkv_wk[j],
                               rwkv_wv[j], rwkv_wo[j], rwkv_w0[j], rwkv_w1[j], rwkv_w2[j], rwkv_a0[j],
                               rwkv_a1[j], rwkv_a2[j], rwkv_g1[j], rwkv_g2[j], rwkv_k_k[j], rwkv_k_a[j],
                               rwkv_r_k[j], rwkv_lnx_g[j], rwkv_lnx_b[j])
            new_s.append(ns)
        x = peer_layer(x, mods, lay, norm2[i], peer_wq[i], peer_keys[i], peer_u[i], peer_v[i])
    y = rmsnorm(x, final_norm)
    nc = n_ctx * ctx_seq
    return (y[:nc].reshape(n_ctx, ctx_seq, d), y[nc:].reshape(n_lat, lat_seq, d),
            jnp.stack(new_k, axis=1), jnp.stack(new_v, axis=1), jnp.stack(new_s, axis=1))
```

```python
import collections
import functools
import math

import jax
import jax.numpy as jnp
from jax import lax
from jax.experimental import pallas as pl
from jax.experimental.pallas import tpu as pltpu

F32 = jnp.float32
BF16 = jnp.bfloat16
HIGHEST = lax.Precision.HIGHEST

HEAD_DIM = 128
KV_GROUP = 4
ROPE_THETA = 10000.0
GRID_W = 64
CONV_WIDTH = 31
RWKV_HEAD = 64
GN_EPS = 64e-5
N_KEYS = 128
PEER_HEADS = 8
PEER_TOPK = 16
NORM_EPS = 1e-6
LN_EPS = 1e-5

LANES = 128
SUBLANES = 8
VMEM_LIMIT_BYTES = 56 * 1024 * 1024
MXU_ROWS = 512

WKV_CHUNK = 64
WKV_GROUP = 4

Layout = collections.namedtuple("Layout", "n_ctx ctx_seq n_lat lat_seq")


def _ctx_rows(lay):
    return lay.n_ctx * lay.ctx_seq


def _rows(lay):
    return _ctx_rows(lay) + lay.n_lat * lay.lat_seq


def _cond_of_tile(i, tm, lay):
    n_ctx_tiles = _ctx_rows(lay) // tm
    per = lay.lat_seq // tm
    return jnp.where(i < n_ctx_tiles, 0, 1 + (i - n_ctx_tiles) // per)


def _params(*sem):
    return pltpu.CompilerParams(dimension_semantics=sem, vmem_limit_bytes=VMEM_LIMIT_BYTES)


def _tile(n, pref):
    t = min(n, pref)
    while n % t:
        t //= 2
    assert t == n or t % LANES == 0, (n, pref)
    return t


def _mod_spec(which, tm, lay, d, ngrid):
    if ngrid == 1:
        return pl.BlockSpec((1, 1, d), lambda i: (_cond_of_tile(i, tm, lay) * 6 + which, 0, 0))
    return pl.BlockSpec((1, 1, d), lambda i, j: (_cond_of_tile(i, tm, lay) * 6 + which, 0, 0))


def _rms_mod(x, g, shift, scale):
    ms = jnp.mean(x * x, axis=-1, keepdims=True)
    return (x * lax.rsqrt(ms + NORM_EPS) * g) * (1.0 + scale) + shift


def _modulate_kernel(x_ref, g_ref, sh_ref, sc_ref, o_ref, *, transpose):
    h = _rms_mod(x_ref[...], g_ref[...], sh_ref[0], sc_ref[0])
    if transpose:
        o_ref[...] = h.T.astype(o_ref.dtype)
    else:
        o_ref[...] = h.astype(o_ref.dtype)


def modulate(x, g, mods, which_shift, lay, *, out_dtype=BF16, transpose=False, tm=256):
    t, d = x.shape
    tm = _tile(math.gcd(lay.ctx_seq * lay.n_ctx, lay.lat_seq), tm)
    out_shape = (d, t) if transpose else (t, d)
    out_spec = pl.BlockSpec((d, tm), lambda i: (0, i)) if transpose else pl.BlockSpec((tm, d), lambda i: (i, 0))
    return pl.pallas_call(
        functools.partial(_modulate_kernel, transpose=transpose),
        grid=(t // tm,),
        in_specs=[pl.BlockSpec((tm, d), lambda i: (i, 0)),
                  pl.BlockSpec((1, d), lambda i: (0, 0)),
                  _mod_spec(which_shift, tm, lay, d, 1),
                  _mod_spec(which_shift + 1, tm, lay, d, 1)],
        out_specs=out_spec,
        out_shape=jax.ShapeDtypeStruct(out_shape, out_dtype),
        compiler_params=_params("parallel"),
        name="modulate_t" if transpose else "modulate",
    )(x, g.reshape(1, d), mods, mods)


def _linear_kernel(*refs, has_bias, act, glu, resid, mod):
    it = iter(refs)
    a_ref = next(it)
    w_ref = next(it)
    w2_ref = next(it) if glu else None
    b_ref = next(it) if has_bias else None
    b2_ref = next(it) if (glu and has_bias) else None
    x_ref = next(it) if resid else None
    gt_ref = next(it) if resid else None
    g_ref = next(it) if mod else None
    sh_ref = next(it) if mod else None
    sc_ref = next(it) if mod else None
    o_ref = next(it)
    if mod:
        h_scr = next(it)

        @pl.when(pl.program_id(1) == 0)
        def _():
            h_scr[...] = _rms_mod(a_ref[...], g_ref[...], sh_ref[0], sc_ref[0]).astype(BF16)

        a = h_scr[...]
    else:
        a = a_ref[...].astype(BF16)
    acc = jnp.dot(a, w_ref[...], preferred_element_type=F32)
    if has_bias:
        acc = acc + b_ref[...]
    if glu:
        acc2 = jnp.dot(a, w2_ref[...], preferred_element_type=F32)
        if has_bias:
            acc2 = acc2 + b2_ref[...]
        acc = acc * jax.nn.sigmoid(acc2)
    if act == "sigmoid":
        acc = jax.nn.sigmoid(acc)
    elif act == "tanh":
        acc = jnp.tanh(acc)
    if resid:
        acc = x_ref[...] + gt_ref[0] * acc
    o_ref[...] = acc.astype(o_ref.dtype)


def linear(a, w, *, bias=None, act=None, glu=False, resid=None, mod=None, out_dtype=F32, tm=1024, tn=512,
           name="linear"):
    m, k = a.shape
    n = w.shape[1] // 2 if glu else w.shape[1]
    for opt in (resid, mod):
        if opt is not None:
            tm = _tile(math.gcd(_ctx_rows(opt[3]), opt[3].lat_seq), tm)
    tm = _tile(m, tm)
    tn = _tile(n, tn)
    nj = n // tn
    in_specs = [pl.BlockSpec((tm, k), lambda i, j: (i, 0)),
                pl.BlockSpec((k, tn), lambda i, j: (0, j))]
    args = [a, w]
    if glu:
        in_specs.append(pl.BlockSpec((k, tn), lambda i, j: (0, j + nj)))
        args.append(w)
    if bias is not None:
        b2d = bias.reshape(1, -1).astype(F32)
        in_specs.append(pl.BlockSpec((1, tn), lambda i, j: (0, j)))
        args.append(b2d)
        if glu:
            in_specs.append(pl.BlockSpec((1, tn), lambda i, j: (0, j + nj)))
            args.append(b2d)
    if resid is not None:
        x, mods, which, lay = resid
        in_specs.append(pl.BlockSpec((tm, tn), lambda i, j: (i, j)))
        args.append(x)
        in_specs.append(pl.BlockSpec((1, 1, tn), lambda i, j: (_cond_of_tile(i, tm, lay) * 6 + which, 0, j)))
        args.append(mods)
    scratch = []
    if mod is not None:
        g, mmods, which_shift, mlay = mod
        in_specs.append(pl.BlockSpec((1, k), lambda i, j: (0, 0)))
        args.append(g.reshape(1, k))
        for off in (0, 1):
            in_specs.append(_mod_spec(which_shift + off, tm, mlay, k, 2))
            args.append(mmods)
        scratch.append(pltpu.VMEM((tm, k), BF16))
    return pl.pallas_call(
        functools.partial(_linear_kernel, has_bias=bias is not None, act=act, glu=glu, resid=resid is not None,
                          mod=mod is not None),
        grid=(m // tm, nj),
        in_specs=in_specs,
        out_specs=pl.BlockSpec((tm, tn), lambda i, j: (i, j)),
        out_shape=jax.ShapeDtypeStruct((m, n), out_dtype),
        scratch_shapes=scratch,
        compiler_params=_params("parallel", "arbitrary" if mod is not None else "parallel"),
        name=name,
    )(*args)


def rope_tables(lat_seq):
    rows = lat_seq // GRID_W
    pos_row = jnp.repeat(jnp.arange(rows, dtype=F32), GRID_W)
    pos_col = jnp.tile(jnp.arange(GRID_W, dtype=F32), rows)
    n_freq = HEAD_DIM // 4
    inv = ROPE_THETA ** (-jnp.arange(n_freq, dtype=F32) / n_freq)
    ang_r = pos_row[:, None] * inv
    ang_c = pos_col[:, None] * inv
    cos = jnp.concatenate([jnp.cos(ang_r), jnp.cos(ang_r), jnp.cos(ang_c), jnp.cos(ang_c)], axis=1)
    sin = jnp.concatenate([-jnp.sin(ang_r), jnp.sin(ang_r), -jnp.sin(ang_c), jnp.sin(ang_c)], axis=1)
    return cos, sin


def _head_rms(x, g):
    ms = jnp.mean(x * x, axis=-1, keepdims=True)
    return x * lax.rsqrt(ms + NORM_EPS) * g


def _rope(x, cos, sin):
    q = HEAD_DIM // 4
    lane = lax.broadcasted_iota(jnp.int32, x.shape, 1)
    first = (lane // q) % 2 == 0
    partner = jnp.where(first, pltpu.roll(x, HEAD_DIM - q, 1), pltpu.roll(x, q, 1))
    return x * cos + partner * sin


def _qkv_post_kernel(*refs, n_heads, n_kv, rope, emit_f32):
    it = iter(refs)
    qkv_ref = next(it)
    qn_ref = next(it)
    kn_ref = next(it)
    cos_ref = next(it) if rope else None
    sin_ref = next(it) if rope else None
    q_ref = next(it)
    k_ref = next(it)
    v_ref = next(it)
    kf_ref = next(it) if emit_f32 else None
    vf_ref = next(it) if emit_f32 else None
    for h in range(n_heads + n_kv):
        x = qkv_ref[:, h * HEAD_DIM:(h + 1) * HEAD_DIM]
        is_q = h < n_heads
        y = _head_rms(x, qn_ref[...] if is_q else kn_ref[...])
        if emit_f32 and not is_q:
            kf_ref[:, (h - n_heads) * HEAD_DIM:(h - n_heads + 1) * HEAD_DIM] = y
        if rope:
            y = _rope(y, cos_ref[...], sin_ref[...])
        if is_q:
            q_ref[:, h * HEAD_DIM:(h + 1) * HEAD_DIM] = y.astype(q_ref.dtype)
        else:
            k_ref[:, (h - n_heads) * HEAD_DIM:(h - n_heads + 1) * HEAD_DIM] = y.astype(k_ref.dtype)
    v = qkv_ref[:, (n_heads + n_kv) * HEAD_DIM:]
    v_ref[...] = v.astype(v_ref.dtype)
    if emit_f32:
        vf_ref[...] = v


def qkv_post(qkv, q_norm, k_norm, *, row0, nrows, seq, rope_tabs, emit_f32, n_heads, tm=256):
    n_kv = n_heads // KV_GROUP
    dq, dk = n_heads * HEAD_DIM, n_kv * HEAD_DIM
    tm = _tile(seq, tm)
    assert row0 % tm == 0
    r0 = row0 // tm
    rope = rope_tabs is not None
    in_specs = [pl.BlockSpec((tm, dq + 2 * dk), lambda i: (i + r0, 0)),
                pl.BlockSpec((1, HEAD_DIM), lambda i: (0, 0)),
                pl.BlockSpec((1, HEAD_DIM), lambda i: (0, 0))]
    args = [qkv, q_norm.reshape(1, HEAD_DIM), k_norm.reshape(1, HEAD_DIM)]
    if rope:
        per = seq // tm
        in_specs += [pl.BlockSpec((tm, HEAD_DIM), lambda i: (i % per, 0))] * 2
        args += list(rope_tabs)
    out_shape = [jax.ShapeDtypeStruct((nrows, dq), BF16), jax.ShapeDtypeStruct((nrows, dk), BF16),
                 jax.ShapeDtypeStruct((nrows, dk), BF16)]
    out_specs = [pl.BlockSpec((tm, dq), lambda i: (i, 0)), pl.BlockSpec((tm, dk), lambda i: (i, 0)),
                 pl.BlockSpec((tm, dk), lambda i: (i, 0))]
    if emit_f32:
        out_shape += [jax.ShapeDtypeStruct((nrows, dk), F32)] * 2
        out_specs += [pl.BlockSpec((tm, dk), lambda i: (i, 0))] * 2
    return pl.pallas_call(
        functools.partial(_qkv_post_kernel, n_heads=n_heads, n_kv=n_kv, rope=rope, emit_f32=emit_f32),
        grid=(nrows // tm,),
        in_specs=in_specs, out_specs=out_specs, out_shape=out_shape,
        compiler_params=_params("parallel"),
        name="qkv_post_rope" if rope else "qkv_post",
    )(*args)


def _attn_kernel(*refs, has_cache, has_prev):
    it = iter(refs)
    q_ref = next(it)
    k_ref = next(it)
    v_ref = next(it)
    kc_ref = next(it) if has_cache else None
    vc_ref = next(it) if has_cache else None
    if has_prev:
        next(it)
    o_ref = next(it)
    c = HEAD_DIM ** -0.5 * math.log2(math.e)
    nt = (((1,), (1,)), ((), ()))
    k = k_ref[...]
    v = v_ref[...]
    for h in range(KV_GROUP):
        q = q_ref[:, h * HEAD_DIM:(h + 1) * HEAD_DIM]
        s = lax.dot_general(q, k, nt, preferred_element_type=F32)
        m = jnp.max(s, axis=-1, keepdims=True)
        if has_cache:
            s2 = lax.dot_general(q, kc_ref[0], nt, preferred_element_type=F32)
            m = jnp.maximum(m, jnp.max(s2, axis=-1, keepdims=True))
        p = jnp.exp2((s - m) * c)
        l = jnp.sum(p, axis=-1, keepdims=True)
        o = jnp.dot(p.astype(BF16), v, preferred_element_type=F32)
        if has_cache:
            p2 = jnp.exp2((s2 - m) * c)
            l = l + jnp.sum(p2, axis=-1, keepdims=True)
            o = o + jnp.dot(p2.astype(BF16), vc_ref[0], preferred_element_type=F32)
        o_ref[:, h * HEAD_DIM:(h + 1) * HEAD_DIM] = (o / l).astype(o_ref.dtype)


def attention(q, k, v, cache_k, cache_v, *, nbatch, seq, out_rows, row0, prev, tq=256):
    n_kv = k.shape[1] // HEAD_DIM
    tq = _tile(seq, tq)
    nq = seq // tq
    assert row0 % tq == 0
    r0 = row0 // tq
    gw = KV_GROUP * HEAD_DIM
    has_cache = cache_k is not None
    in_specs = [pl.BlockSpec((tq, gw), lambda b, g, i: (b * nq + i, g)),
                pl.BlockSpec((seq, HEAD_DIM), lambda b, g, i: (b, g)),
                pl.BlockSpec((seq, HEAD_DIM), lambda b, g, i: (b, g))]
    args = [q, k, v]
    if has_cache:
        past = cache_k.shape[1]
        in_specs += [pl.BlockSpec((1, past, HEAD_DIM), lambda b, g, i: (b, 0, g))] * 2
        args += [cache_k, cache_v]
    aliases = {}
    if prev is not None:
        aliases = {len(args): 0}
        in_specs.append(pl.BlockSpec(memory_space=pl.ANY))
        args.append(prev)
    return pl.pallas_call(
        functools.partial(_attn_kernel, has_cache=has_cache, has_prev=prev is not None),
        grid=(nbatch, n_kv, nq),
        in_specs=in_specs,
        out_specs=pl.BlockSpec((tq, gw), lambda b, g, i: (r0 + b * nq + i, g)),
        out_shape=jax.ShapeDtypeStruct((out_rows, q.shape[1]), BF16),
        input_output_aliases=aliases,
        compiler_params=_params("parallel", "parallel", "parallel"),
        name="attention_cache" if has_cache else "attention",
    )(*args)


def _seq_edges(i, tm, lay):
    n_ctx_tiles = _ctx_rows(lay) // tm
    per_c = lay.ctx_seq // tm
    per_l = lay.lat_seq // tm
    il = i - n_ctx_tiles
    first = jnp.where(i < n_ctx_tiles, i % per_c == 0, il % per_l == 0)
    last = jnp.where(i < n_ctx_tiles, i % per_c == per_c - 1, il % per_l == per_l - 1)
    return first, last


def _halo_specs(tm, halo, d, nblocks_total):
    r = tm // halo
    prev = pl.BlockSpec((halo, d), lambda i: (jnp.maximum(i * r - 1, 0), 0))
    nxt = pl.BlockSpec((halo, d), lambda i: (jnp.minimum((i + 1) * r, nblocks_total - 1), 0))
    return prev, nxt


CONV_HALO = 16


def _dwconv_kernel(u_ref, up_ref, un_ref, dw_ref, dwb_ref, g_ref, b_ref, o_ref, ext_ref, acc_ref, *, tm, lay, cb):
    i = pl.program_id(0)
    first, last = _seq_edges(i, tm, lay)
    d = u_ref.shape[1]
    ext_ref[CONV_HALO:CONV_HALO + tm, :] = u_ref[...]
    ext_ref[0:CONV_HALO, :] = jnp.where(first, 0.0, up_ref[...])
    ext_ref[CONV_HALO + tm:, :] = jnp.where(last, 0.0, un_ref[...])
    pad = CONV_WIDTH // 2

    def col_block(c, carry):
        cs = pl.multiple_of(c * cb, cb)
        w = ext_ref[:, pl.ds(cs, cb)]
        taps = dw_ref[:, pl.ds(cs, cb)]
        acc = jnp.zeros((tm, cb), F32)
        for j in range(CONV_WIDTH):
            o = CONV_HALO - pad + j
            acc = acc + w[o:o + tm] * taps[j:j + 1]
        acc_ref[:, pl.ds(cs, cb)] = acc
        return carry

    lax.fori_loop(0, d // cb, col_block, 0)
    y = acc_ref[...] + dwb_ref[...]
    mu = jnp.mean(y, axis=-1, keepdims=True)
    yc = y - mu
    var = jnp.mean(yc * yc, axis=-1, keepdims=True)
    z = yc * lax.rsqrt(var + LN_EPS) * g_ref[...] + b_ref[...]
    o_ref[...] = (z * jax.nn.sigmoid(z)).astype(o_ref.dtype)


def dwconv_ln_silu(u, dw, dw_b, ln_g, ln_b, lay, *, tm=128, cb=256):
    t, d = u.shape
    tm = _tile(math.gcd(lay.ctx_seq, lay.lat_seq), tm)
    cb = _tile(d, cb)
    prev, nxt = _halo_specs(tm, CONV_HALO, d, t // CONV_HALO)
    dwp = jnp.zeros((32, d), F32).at[:CONV_WIDTH].set(dw)
    row = lambda a: a.reshape(1, d)
    const = lambda r: pl.BlockSpec((r, d), lambda i: (0, 0))
    return pl.pallas_call(
        functools.partial(_dwconv_kernel, tm=tm, lay=lay, cb=cb),
        grid=(t // tm,),
        in_specs=[pl.BlockSpec((tm, d), lambda i: (i, 0)), prev, nxt,
                  const(32), const(1), const(1), const(1)],
        out_specs=pl.BlockSpec((tm, d), lambda i: (i, 0)),
        out_shape=jax.ShapeDtypeStruct((t, d), BF16),
        scratch_shapes=[pltpu.VMEM((tm + 2 * CONV_HALO, d), F32), pltpu.VMEM((tm, d), F32)],
        compiler_params=_params("parallel"),
        name="dwconv_ln_silu",
    )(u, u, u, dwp, row(dw_b), row(ln_g), row(ln_b))


MIX_HALO = 8


def _rwkv_mix_kernel(x_ref, xp_ref, xn_ref, g_ref, sh_ref, sc_ref, mu_ref, *o_refs, tm, lay):
    i = pl.program_id(0)
    first, last = _seq_edges(i, tm, lay)
    g, sh, sc = g_ref[...], sh_ref[0], sc_ref[0]
    h = _rms_mod(x_ref[...], g, sh, sc)
    hp = _rms_mod(xp_ref[MIX_HALO - 1:MIX_HALO, :], g, sh, sc)
    hn = _rms_mod(xn_ref[0:1, :], g, sh, sc)
    hp = jnp.where(first, 0.0, hp)
    hn = jnp.where(last, 0.0, hn)
    row = lax.broadcasted_iota(jnp.int32, h.shape, 0)
    h_prev = jnp.where(row == 0, hp, pltpu.roll(h, 1, 0))
    h_next = jnp.where(row == tm - 1, hn, pltpu.roll(h, tm - 1, 0))
    xx = 0.5 * (h_prev + h_next) - h
    for n, o_ref in enumerate(o_refs):
        o_ref[...] = (h + xx * mu_ref[n:n + 1, :]).astype(o_ref.dtype)


def rwkv_mix(x, g, mods, mu, lay, *, tm=256):
    t, d = x.shape
    tm = _tile(math.gcd(lay.ctx_seq, lay.lat_seq), tm)
    prev, nxt = _halo_specs(tm, MIX_HALO, d, t // MIX_HALO)
    mup = jnp.zeros((8, d), F32).at[:6].set(mu)
    return pl.pallas_call(
        functools.partial(_rwkv_mix_kernel, tm=tm, lay=lay),
        grid=(t // tm,),
        in_specs=[pl.BlockSpec((tm, d), lambda i: (i, 0)), prev, nxt,
                  pl.BlockSpec((1, d), lambda i: (0, 0)),
                  _mod_spec(0, tm, lay, d, 1), _mod_spec(1, tm, lay, d, 1),
                  pl.BlockSpec((8, d), lambda i: (0, 0))],
        out_specs=[pl.BlockSpec((tm, d), lambda i: (i, 0))] * 6,
        out_shape=[jax.ShapeDtypeStruct((t, d), BF16)] * 6,
        compiler_params=_params("parallel"),
        name="rwkv_mix",
    )(x, x, x, g.reshape(1, d), mods, mods, mup)


def _split_dot(x, w_bf16, passes, *, w_left=False):
    acc = None
    rem = x
    for _ in range(passes):
        hi = rem.astype(BF16)
        part = (jnp.dot(w_bf16, hi, preferred_element_type=F32) if w_left
                else jnp.dot(hi, w_bf16, preferred_element_type=F32))
        acc = part if acc is None else acc + part
        rem = rem - hi.astype(F32)
    return acc


def _wkv_groups(r, k, v, wl, al, kkp, ka, rk, s_old, *, reverse, masks):
    C = WKV_CHUNK
    N = RWKV_HEAD
    bd_f32, bd_bf16, tri, strict, incl, eye, off_masks = masks
    each = lambda f, *xs: [f(*a) for a in zip(*xs)]

    def bd(x):
        return jnp.concatenate([x.astype(BF16)] * WKV_GROUP, axis=0) * bd_bf16

    nt = (((1,), (1,)), ((), ()))
    dot_nt = lambda x, y: lax.dot_general(x.astype(BF16), y, nt, preferred_element_type=F32)
    dot_nn = lambda x, y: jnp.dot(x.astype(BF16), y, preferred_element_type=F32)

    iclr = each(jax.nn.sigmoid, al)
    lw = each(lambda w: -jnp.exp(-(jnp.maximum(-w, 0.0) + jnp.log1p(jnp.exp(-jnp.abs(w)))) - 0.5), wl)
    kx = each(lambda k_, p: k_ * p, k, kkp)
    kn = each(lambda x: _split_dot(x * x, bd_bf16, 2), kx)
    kk = each(lambda x, n: x / jnp.maximum(jnp.sqrt(n), 1e-12), kx, kn)
    kd = each(lambda k_, i, p: k_ * (1.0 + (i - 1.0) * p), k, iclr, ka)
    bsum = each(lambda r_, d_, p: _split_dot(r_ * d_ * p, bd_bf16, 2), r, kd, rk)
    bonus = each(lambda s, v_: s * v_, bsum, v)
    cum = each(lambda l: _split_dot(l, tri, 3, w_left=True), lw)
    g_in = each(jnp.exp, cum)
    g_inv = each(lambda c_: jnp.exp(-c_), cum)
    at = each(lambda kk_, c_, l: (-kk_ * jnp.exp(c_ - l)).astype(BF16), kk, cum, lw)
    rt = each(lambda r_, g: (r_ * g).astype(BF16), r, g_in)
    bt = each(lambda kk_, i, g: (kk_ * i * g).astype(BF16), kk, iclr, g_inv)
    kt = each(lambda d_, g: (d_ * g).astype(BF16), kd, g_inv)

    ar = each(lambda a_, r_: jnp.concatenate([a_, r_], axis=0), at, rt)
    p_b = each(lambda x, y: dot_nt(x, bd(y)), ar, bt)
    p_k = each(lambda x, y: dot_nt(x, bd(y)), ar, kt)
    p_s = each(lambda x, y: dot_nt(x, bd(y)), ar, s_old)
    bdv = each(bd, v)
    lp = each(lambda p: jnp.where(strict, p[:C], 0.0), p_b)
    rhs = each(lambda ps, pk, w: ps[:C] + dot_nn(jnp.where(strict, pk[:C], 0.0), w), p_s, p_k, bdv)
    tinv = each(lambda l: eye + jnp.where(off_masks[0], l, 0.0), lp)
    for off in off_masks[1:]:
        x = each(lambda t, l: dot_nn(t, bd(jnp.where(off, l, 0.0))), tinv, lp)
        tinv = each(lambda t, x_: t + dot_nn(x_, bd(t)), tinv, x)
    u = each(lambda t, x: dot_nn(t, bd(x)), tinv, rhs)
    y = each(lambda ps, pb, pk, u_, w: ps[C:] + dot_nn(jnp.where(incl, pb[C:], 0.0), bd(u_))
             + dot_nn(jnp.where(incl, pk[C:], 0.0), w), p_s, p_b, p_k, u, bdv)

    def new_state(u_, v_, b_, k_, s, g):
        uv = jnp.concatenate([u_, v_], axis=0)
        bk = jnp.concatenate([b_, k_], axis=0)
        f = dot_nn(uv.T, bk) * bd_f32
        zsum = f[0:N]
        for h in range(1, WKV_GROUP):
            zsum = zsum + f[h * N:(h + 1) * N]
        return (s + zsum) * (g[0:1] if reverse else g[C - 1:C])

    return y, bonus, each(new_state, u, v, bt, kt, s_old, g_in)


def _wkv_kernel(r_ref, k_ref, v_ref, wl_ref, al_ref, kkp_ref, ka_ref, rk_ref, s0_ref, *rest, reverse, nc, nsub):
    y_ref, bo_ref, sf_ref, s_scr = rest[-4:]
    C = WKV_CHUNK
    N = RWKV_HEAD
    W = WKV_GROUP * N
    c = pl.program_id(2)

    @pl.when(c == 0)
    def _():
        s_scr[...] = s0_ref[0, 0]

    lane_i = lax.broadcasted_iota(jnp.int32, (W, W), 0)
    lane_j = lax.broadcasted_iota(jnp.int32, (W, W), 1)
    bd_f32 = ((lane_i // N) == (lane_j // N)).astype(F32)
    tt = lax.broadcasted_iota(jnp.int32, (C, C), 0)
    ss = lax.broadcasted_iota(jnp.int32, (C, C), 1)
    tri = ((ss >= tt) if reverse else (ss <= tt)).astype(BF16)
    t2 = lax.broadcasted_iota(jnp.int32, (C, W), 0)
    s2 = lax.broadcasted_iota(jnp.int32, (C, W), 1) % C
    strict = (s2 > t2) if reverse else (s2 < t2)
    incl = (s2 >= t2) if reverse else (s2 <= t2)
    eye = (s2 == t2).astype(F32)
    off_masks = []
    b = 1
    while b < C:
        lo, hi = (s2, t2) if reverse else (t2, s2)
        off_masks.append(((t2 // (2 * b)) == (s2 // (2 * b))) & ((lo % (2 * b)) >= b) & ((hi % (2 * b)) < b))
        b *= 2
    masks = (bd_f32, bd_f32.astype(BF16), tri, strict, incl, eye, off_masks)

    sls = [slice(gi * W, (gi + 1) * W) for gi in range(nsub)]
    groups = lambda ref: [ref[:, sl] for sl in sls]
    y, bonus, s_new = _wkv_groups(*map(groups, (r_ref, k_ref, v_ref, wl_ref, al_ref, kkp_ref, ka_ref, rk_ref, s_scr)),
                                  reverse=reverse, masks=masks)
    for gi, sl in enumerate(sls):
        y_ref[:, sl] = y[gi]
        bo_ref[:, sl] = bonus[gi]
        s_scr[:, sl] = s_new[gi]

    @pl.when(c == nc - 1)
    def _():
        sf_ref[0, 0] = s_scr[...]


def wkv_scan(r, k, v, wl, al, k_k, k_a, r_k, s0, *, row0, nbatch, seq, reverse, nsub, prev=None):
    d = r.shape[1]
    C, W = WKV_CHUNK, WKV_GROUP * RWKV_HEAD * nsub
    assert seq % C == 0 and row0 % C == 0 and d % W == 0
    nc, ng, rb0 = seq // C, d // W, row0 // C

    def chunk(c):
        return (nc - 1 - c) if reverse else c

    tok_in = pl.BlockSpec((C, W), lambda b, g, c: (rb0 + b * nc + chunk(c), g))
    par = pl.BlockSpec((1, W), lambda b, g, c: (0, g))
    st = pl.BlockSpec((1, 1, RWKV_HEAD, W), lambda b, g, c: (b, g, 0, 0))
    row = lambda x: x.reshape(1, d)
    in_specs = [tok_in] * 5 + [par] * 3 + [st]
    args = [r, k, v, wl, al, row(k_k), row(k_a), row(r_k), s0]
    aliases = {}
    if prev is not None:
        aliases = {len(args): 0, len(args) + 1: 1}
        in_specs += [pl.BlockSpec(memory_space=pl.ANY)] * 2
        args += list(prev)
    return pl.pallas_call(
        functools.partial(_wkv_kernel, reverse=reverse, nc=nc, nsub=nsub),
        grid=(nbatch, ng, nc),
        in_specs=in_specs,
        out_specs=[tok_in, tok_in, st],
        out_shape=[jax.ShapeDtypeStruct(r.shape, F32), jax.ShapeDtypeStruct(r.shape, F32),
                   jax.ShapeDtypeStruct(s0.shape, F32)],
        input_output_aliases=aliases,
        scratch_shapes=[pltpu.VMEM((RWKV_HEAD, W), F32)],
        compiler_params=_params("parallel", "parallel", "arbitrary"),
        name="wkv_scan_rev" if reverse else "wkv_scan_fwd",
    )(*args)


def _rwkv_out_kernel(y0_ref, y1_ref, b0_ref, b1_ref, gate_ref, g_ref, b_ref, o_ref):
    d = y0_ref.shape[1]
    li = lax.broadcasted_iota(jnp.int32, (LANES, LANES), 0)
    lj = lax.broadcasted_iota(jnp.int32, (LANES, LANES), 1)
    avg = ((li // RWKV_HEAD) == (lj // RWKV_HEAD)).astype(F32) * (1.0 / RWKV_HEAD)
    for c in range(d // LANES):
        sl = slice(c * LANES, (c + 1) * LANES)
        y = y0_ref[:, sl] + y1_ref[:, sl]
        mu = jnp.dot(y, avg, precision=HIGHEST, preferred_element_type=F32)
        yc = y - mu
        var = jnp.dot(yc * yc, avg, precision=HIGHEST, preferred_element_type=F32)
        yn = yc * lax.rsqrt(var + GN_EPS) * g_ref[:, sl] + b_ref[:, sl]
        o_ref[:, sl] = ((yn + b0_ref[:, sl] + b1_ref[:, sl]) * gate_ref[:, sl]).astype(o_ref.dtype)


def rwkv_out(y0, y1, b0, b1, gate, lnx_g, lnx_b, *, tm=256):
    t, d = y0.shape
    tm = _tile(t, tm)
    tok = pl.BlockSpec((tm, d), lambda i: (i, 0))
    par = pl.BlockSpec((1, d), lambda i: (0, 0))
    return pl.pallas_call(
        _rwkv_out_kernel,
        grid=(t // tm,),
        in_specs=[tok] * 5 + [par] * 2,
        out_specs=tok,
        out_shape=jax.ShapeDtypeStruct((t, d), BF16),
        compiler_params=_params("parallel"),
        name="rwkv_out",
    )(y0, y1, b0, b1, gate, lnx_g.reshape(1, d), lnx_b.reshape(1, d))


def _extract_desc(x, n):
    vals = []
    cur = x
    for _ in range(n):
        mx = jnp.max(cur, axis=0, keepdims=True)
        vals.append(jnp.maximum(mx, 0.0))
        cur = jnp.where(cur == mx, -1.0, cur)
    return vals


def _peer_route_kernel(q_ref, keys_ref, ea_ref, e1_ref, th_ref):
    kp = PEER_TOPK
    half = kp // 2
    ths = []
    for h in range(PEER_HEADS):
        es = []
        tops = []
        for p in range(2):
            idx = h * 2 + p
            qhp = q_ref[idx * N_KEYS:(idx + 1) * N_KEYS, :]
            s = jnp.dot(keys_ref[idx], qhp, precision=HIGHEST, preferred_element_type=F32)
            e = jnp.exp(s - jnp.max(s, axis=0, keepdims=True))
            es.append(e)
            tops.append(jnp.concatenate(_extract_desc(e, kp), axis=0))
        v0, v1 = tops

        def cands(a0):
            blk = [a0[a:a + 1] * v1[:half] for a in range(half)]
            blk.append(a0[0:1] * v1[half:])
            blk.append(a0[half:] * v1[0:1])
            return jnp.concatenate(blk, axis=0)

        cand = cands(v0)
        best = _extract_desc(cand, kp)
        zsum = best[0]
        for b in best[1:]:
            zsum = zsum + b
        inv_z = 0.5 / zsum
        cand_n = cands(v0 * inv_z)
        th = jnp.min(jnp.where(cand >= best[kp - 1], cand_n, jnp.inf), axis=0, keepdims=True)
        ths.append(th)
        ea_ref[h * N_KEYS:(h + 1) * N_KEYS, :] = es[0] * inv_z
        e1_ref[h * N_KEYS:(h + 1) * N_KEYS, :] = es[1]
    th_ref[...] = jnp.concatenate(ths, axis=0)


def peer_route(q_t, keys, *, tn=256):
    rows, t = q_t.shape
    tn = _tile(t, tn)
    hk = PEER_HEADS * N_KEYS
    return pl.pallas_call(
        _peer_route_kernel,
        grid=(t // tn,),
        in_specs=[pl.BlockSpec((rows, tn), lambda i: (0, i)),
                  pl.BlockSpec(keys.shape, lambda i: (0, 0, 0))],
        out_specs=[pl.BlockSpec((hk, tn), lambda i: (0, i)), pl.BlockSpec((hk, tn), lambda i: (0, i)),
                   pl.BlockSpec((PEER_HEADS, tn), lambda i: (0, i))],
        out_shape=[jax.ShapeDtypeStruct((hk, t), F32), jax.ShapeDtypeStruct((hk, t), F32),
                   jax.ShapeDtypeStruct((PEER_HEADS, t), F32)],
        compiler_params=_params("parallel"),
        name="peer_route",
    )(q_t, keys)


def _peer_dense_kernel(h_ref, u_ref, vt_ref, ea_ref, e1_ref, th_ref, x_ref, gt_ref, o_ref, acc_ref, w_ref, *, ni):
    e = pl.program_id(1)

    @pl.when(e == 0)
    def _():
        acc_ref[...] = jnp.zeros_like(acc_ref)

    hv = h_ref[...]
    a = jnp.concatenate([jnp.dot(u_ref[r0:r0 + MXU_ROWS, :], hv, preferred_element_type=F32)
                         for r0 in range(0, u_ref.shape[0], MXU_ROWS)], axis=0)
    tn = h_ref.shape[1]
    lw = min(tn, 2 * LANES)
    for l0 in range(0, tn, lw):
        ls = slice(l0, l0 + lw)
        ths = [th_ref[h:h + 1, ls] for h in range(PEER_HEADS)]
        for i8 in range(ni // SUBLANES):
            base = pl.multiple_of(e * ni + i8 * SUBLANES, SUBLANES)
            tiles = [ea_ref[pl.ds(h * N_KEYS + base, SUBLANES), ls] for h in range(PEER_HEADS)]
            for ii in range(SUBLANES):
                rows = [t[ii:ii + 1] for t in tiles]
                for rb in range(N_KEYS // SUBLANES):
                    w = None
                    for h in range(PEER_HEADS):
                        r0 = h * N_KEYS + rb * SUBLANES
                        p = rows[h] * e1_ref[r0:r0 + SUBLANES, ls]
                        sel = jnp.where(p >= ths[h], p, 0.0)
                        w = sel if w is None else w + sel
                    w0 = (i8 * SUBLANES + ii) * N_KEYS + rb * SUBLANES
                    w_ref[w0:w0 + SUBLANES, ls] = w
    act = a * (1.0 + lax.erf(a * (2.0 ** -0.5)))
    gw = (w_ref[...] * act).astype(BF16)
    for r0 in range(0, acc_ref.shape[0], MXU_ROWS):
        rs = slice(r0, r0 + MXU_ROWS)
        acc_ref[rs, :] += jnp.dot(vt_ref[rs, :], gw, preferred_element_type=F32)

    @pl.when(e == pl.num_programs(1) - 1)
    def _():
        o_ref[...] = x_ref[...] + gt_ref[0] * acc_ref[...].T


def peer_dense(h_t, u_tab, vt_tab, ea, e1, th, x, mods, which_gate, lay, *, tn=512, te=1024):
    d, t = h_t.shape
    n_exp = u_tab.shape[0]
    tn = _tile(math.gcd(_ctx_rows(lay), lay.lat_seq), tn)
    te = _tile(n_exp, te)
    ni = te // N_KEYS
    assert ni % SUBLANES == 0
    hk = PEER_HEADS * N_KEYS
    route = lambda rows: pl.BlockSpec((rows, tn), lambda i, e: (0, i))
    return pl.pallas_call(
        functools.partial(_peer_dense_kernel, ni=ni),
        grid=(t // tn, n_exp // te),
        in_specs=[pl.BlockSpec((d, tn), lambda i, e: (0, i)),
                  pl.BlockSpec((te, d), lambda i, e: (e, 0)),
                  pl.BlockSpec((d, te), lambda i, e: (0, e)),
                  route(hk), route(hk), route(PEER_HEADS),
                  pl.BlockSpec((tn, d), lambda i, e: (i, 0)),
                  pl.BlockSpec((1, 1, d), lambda i, e: (_cond_of_tile(i, tn, lay) * 6 + which_gate, 0, 0))],
        out_specs=pl.BlockSpec((tn, d), lambda i, e: (i, 0)),
        out_shape=jax.ShapeDtypeStruct((t, d), F32),
        scratch_shapes=[pltpu.VMEM((d, tn), F32), pltpu.VMEM((te, tn), F32)],
        compiler_params=_params("parallel", "arbitrary"),
        name="peer_dense",
    )(h_t, u_tab, vt_tab, ea, e1, th, x, mods)


def _rmsnorm_kernel(x_ref, g_ref, o_ref):
    x = x_ref[...]
    ms = jnp.mean(x * x, axis=-1, keepdims=True)
    o_ref[...] = x * lax.rsqrt(ms + NORM_EPS) * g_ref[...]


def rmsnorm(x, g, *, row0, nrows, tm=512):
    d = x.shape[1]
    tm = _tile(math.gcd(row0, nrows) if row0 else nrows, tm)
    r0 = row0 // tm
    return pl.pallas_call(
        _rmsnorm_kernel,
        grid=(nrows // tm,),
        in_specs=[pl.BlockSpec((tm, d), lambda i: (i + r0, 0)), pl.BlockSpec((1, d), lambda i: (0, 0))],
        out_specs=pl.BlockSpec((tm, d), lambda i: (i, 0)),
        out_shape=jax.ShapeDtypeStruct((nrows, d), F32),
        compiler_params=_params("parallel"),
        name="final_rmsnorm",
    )(x, g.reshape(1, d))


def _pad_cols(w, n):
    return jnp.zeros(w.shape[:-1] + (n,), w.dtype).at[..., :w.shape[-1]].set(w)


def _pad_rows(w, n):
    return jnp.zeros((n,) + w.shape[1:], w.dtype).at[:w.shape[0]].set(w)


def _state_to_groups(s, hg):
    b, h, n, _ = s.shape
    return s.reshape(b, h // hg, hg, n, n).transpose(0, 1, 3, 2, 4).reshape(b, h // hg, n, hg * n)


def _groups_to_state(s, hg):
    b, g, n, _ = s.shape
    return s.reshape(b, g, n, hg, n).transpose(0, 1, 3, 2, 4).reshape(b, g * hg, n, n)


def attention_layer(x, mods, lay, norm_g, wqkv, q_norm, k_norm, wo, cache_k, cache_v):
    t, d = x.shape
    n_heads = d // HEAD_DIM
    n_kv = n_heads // KV_GROUP
    qkv = linear(x, wqkv.astype(BF16), mod=(norm_g, mods, 0, lay), name="attn_qkv")
    nc = _ctx_rows(lay)
    qc, kc, vc, kcf, vcf = qkv_post(qkv, q_norm, k_norm, row0=0, nrows=nc, seq=lay.ctx_seq,
                                    rope_tabs=None, emit_f32=True, n_heads=n_heads)
    ql, kl, vl = qkv_post(qkv, q_norm, k_norm, row0=nc, nrows=t - nc, seq=lay.lat_seq,
                          rope_tabs=rope_tables(lay.lat_seq), emit_f32=False, n_heads=n_heads)
    o = attention(qc, kc, vc, None, None, nbatch=lay.n_ctx, seq=lay.ctx_seq, out_rows=t, row0=0, prev=None)
    past = cache_k.shape[1]
    ck = cache_k.reshape(lay.n_lat, past, n_kv * HEAD_DIM).astype(BF16)
    cv = cache_v.reshape(lay.n_lat, past, n_kv * HEAD_DIM).astype(BF16)
    o = attention(ql, kl, vl, ck, cv, nbatch=lay.n_lat, seq=lay.lat_seq, out_rows=t, row0=nc, prev=o)
    x = linear(o, wo.astype(BF16), resid=(x, mods, 2, lay), name="attn_out")
    new_k = kcf.reshape(lay.n_ctx, lay.ctx_seq, n_kv, HEAD_DIM)
    new_v = vcf.reshape(lay.n_ctx, lay.ctx_seq, n_kv, HEAD_DIM)
    return x, new_k, new_v


def conv_layer(x, mods, lay, norm_g, w1, b1, dw, dw_b, ln_g, ln_b, w2, b2):
    u = linear(x, w1.astype(BF16), bias=b1, glu=True, mod=(norm_g, mods, 0, lay), name="conv_glu")
    z = dwconv_ln_silu(u, dw, dw_b, ln_g, ln_b, lay)
    return linear(z, w2.astype(BF16), bias=b2, resid=(x, mods, 2, lay), name="conv_out")


def rwkv_layer(x, mods, lay, norm_g, state, mu, wr, wk, wv, wo, w0, w1, w2, a0, a1, a2, g1, g2,
               k_k, k_a, r_k, lnx_g, lnx_b):
    t, d = x.shape
    xr, xw, xk, xv, xa, xg = rwkv_mix(x, norm_g, mods, mu, lay)
    bf = lambda w: w.astype(BF16)
    r = linear(xr, bf(wr), name="rwkv_r")
    k = linear(xk, bf(wk), name="rwkv_k")
    v = linear(xv, bf(wv), name="rwkv_v")
    gate = linear(linear(xg, bf(g1), act="sigmoid", out_dtype=BF16, name="rwkv_g1"), bf(g2), name="rwkv_g2")
    lora = LANES * pl.cdiv(w1.shape[-1], LANES)
    nc = _ctx_rows(lay)
    nsub = math.gcd(d // (WKV_GROUP * RWKV_HEAD), 8)
    hg = WKV_GROUP * nsub
    s_zero = jnp.zeros((lay.n_ctx, d // (hg * RWKV_HEAD), RWKV_HEAD, hg * RWKV_HEAD), F32)
    ys, bos, finals = [], [], []
    for dirn, rev in enumerate((False, True)):
        tw = linear(xw, bf(_pad_cols(w1[dirn], lora)), act="tanh", out_dtype=BF16, name="rwkv_w1")
        wl = linear(tw, bf(_pad_rows(w2[dirn], lora)), bias=w0[dirn], name="rwkv_w2")
        ta = linear(xa, bf(_pad_cols(a1[dirn], lora)), out_dtype=BF16, name="rwkv_a1")
        al = linear(ta, bf(_pad_rows(a2[dirn], lora)), bias=a0[dirn], name="rwkv_a2")
        scan = functools.partial(wkv_scan, r, k, v, wl, al, k_k, k_a, r_k.reshape(-1), reverse=rev, nsub=nsub)
        yc, bc, sf = scan(s_zero, row0=0, nbatch=lay.n_ctx, seq=lay.ctx_seq)
        y, bo, _ = scan(_state_to_groups(state[:, dirn], hg), row0=nc, nbatch=lay.n_lat, seq=lay.lat_seq,
                        prev=(yc, bc))
        ys.append(y)
        bos.append(bo)
        finals.append(_groups_to_state(sf, hg))
    o = rwkv_out(ys[0], ys[1], bos[0], bos[1], gate, lnx_g, lnx_b)
    x = linear(o, bf(wo), resid=(x, mods, 2, lay), name="rwkv_out_proj")
    return x, jnp.stack(finals, axis=1)


def peer_layer(x, mods, lay, norm_g, wq, keys, u_tab, v_tab):
    h_t = modulate(x, norm_g, mods, 3, lay, transpose=True)
    q_t = linear(wq.T.astype(BF16), h_t, name="peer_query")
    ea, e1, th = peer_route(q_t, keys.reshape(PEER_HEADS * 2, N_KEYS, -1))
    return peer_dense(h_t, u_tab.astype(BF16), v_tab.T.astype(BF16), ea, e1, th, x, mods, 5, lay)


def kernel(x_prompt, x_sample, cache_k, cache_v, state_wkv, c, c_ctx, norm1, norm2, ada_w, ada_b, attn_wqkv, attn_q_norm, attn_k_norm, attn_wo, conv_w1, conv_b1, conv_dw, conv_dw_b, conv_ln_g, conv_ln_b, conv_w2, conv_b2, rwkv_mu, rwkv_wr, rwkv_wk, rwkv_wv, rwkv_wo, rwkv_w0, rwkv_w1, rwkv_w2, rwkv_a0, rwkv_a1, rwkv_a2, rwkv_g1, rwkv_g2, rwkv_k_k, rwkv_k_a, rwkv_r_k, rwkv_lnx_g, rwkv_lnx_b, peer_wq, peer_keys, peer_u, peer_v, final_norm):
    n_ctx, ctx_seq, d = x_prompt.shape
    n_lat, lat_seq, _ = x_sample.shape
    lay = Layout(n_ctx, ctx_seq, n_lat, lat_seq)
    depth = norm1.shape[0]
    x = jnp.concatenate([x_prompt.reshape(n_ctx * ctx_seq, d), x_sample.reshape(n_lat * lat_seq, d)], axis=0)
    n_cond = 1 + n_lat
    cond = jnp.zeros((SUBLANES * pl.cdiv(n_cond, SUBLANES), d), F32).at[0].set(c_ctx).at[1:n_cond].set(c)
    cond = jax.nn.silu(cond).astype(BF16)
    new_k, new_v, new_s = [], [], []
    for i in range(depth):
        kind, j = i % 3, i // 3
        m = linear(cond, ada_w[i].astype(BF16), bias=ada_b[i], tn=1024, name="ada_mods")
        mods = m[:n_cond].reshape(n_cond * 6, 1, d)
        if kind == 0:
            x, nk, nv = attention_layer(x, mods, lay, norm1[i], attn_wqkv[j], attn_q_norm[j], attn_k_norm[j],
                                        attn_wo[j], cache_k[:, j], cache_v[:, j])
            new_k.append(nk)
            new_v.append(nv)
        elif kind == 1:
            x = conv_layer(x, mods, lay, norm1[i], conv_w1[j], conv_b1[j], conv_dw[j], conv_dw_b[j],
                           conv_ln_g[j], conv_ln_b[j], conv_w2[j], conv_b2[j])
        else:
            x, ns = rwkv_layer(x, mods, lay, norm1[i], state_wkv[:, j], rwkv_mu[j], rwkv_wr[j], rwkv_wk[j],
                               rwkv_wv[j], rwkv_wo[j], rwkv_w0[j], rwkv_w1[j], rwkv_w2[j], rwkv_a0[j],
                               rwkv_a1[j], rwkv_a2[j], rwkv_g1[j], rwkv_g2[j], rwkv_k_k[j], rwkv_k_a[j],
                               rwkv_r_k[j], rwkv_lnx_g[j], rwkv_lnx_b[j])
            new_s.append(ns)
        x = peer_layer(x, mods, lay, norm2[i], peer_wq[i], peer_keys[i], peer_u[i], peer_v[i])
    nc = n_ctx * ctx_seq
    y_ctx = rmsnorm(x, final_norm, row0=0, nrows=nc)
    y_lat = rmsnorm(x, final_norm, row0=nc, nrows=n_lat * lat_seq)
    return (y_ctx.reshape(n_ctx, ctx_seq, d), y_lat.reshape(n_lat, lat_seq, d),
            jnp.stack(new_k, axis=1), jnp.stack(new_v, axis=1), jnp.stack(new_s, axis=1))
```

```python
import collections
import functools
import math

import jax
import jax.numpy as jnp
from jax import lax
from jax.experimental import pallas as pl
from jax.experimental.pallas import tpu as pltpu

F32 = jnp.float32
BF16 = jnp.bfloat16
HIGHEST = lax.Precision.HIGHEST

HEAD_DIM = 128
KV_GROUP = 4
ROPE_THETA = 10000.0
GRID_W = 64
CONV_WIDTH = 31
RWKV_HEAD = 64
GN_EPS = 64e-5
N_KEYS = 128
PEER_HEADS = 8
PEER_TOPK = 16
NORM_EPS = 1e-6
LN_EPS = 1e-5

LANES = 128
SUBLANES = 8
VMEM_LIMIT_BYTES = 56 * 1024 * 1024
MXU_ROWS = 512

WKV_CHUNK = 64
WKV_GROUP = 4

Layout = collections.namedtuple("Layout", "n_ctx ctx_seq n_lat lat_seq")


def _ctx_rows(lay):
    return lay.n_ctx * lay.ctx_seq


def _rows(lay):
    return _ctx_rows(lay) + lay.n_lat * lay.lat_seq


def _cond_of_tile(i, tm, lay):
    n_ctx_tiles = _ctx_rows(lay) // tm
    per = lay.lat_seq // tm
    return jnp.where(i < n_ctx_tiles, 0, 1 + (i - n_ctx_tiles) // per)


def _params(*sem):
    return pltpu.CompilerParams(dimension_semantics=sem, vmem_limit_bytes=VMEM_LIMIT_BYTES)


def _tile(n, pref):
    t = min(n, pref)
    while n % t:
        t //= 2
    assert t == n or t % LANES == 0, (n, pref)
    return t


def _mod_spec(which, tm, lay, d, ngrid):
    if ngrid == 1:
        return pl.BlockSpec((1, 1, d), lambda i: (_cond_of_tile(i, tm, lay) * 6 + which, 0, 0))
    return pl.BlockSpec((1, 1, d), lambda i, j: (_cond_of_tile(i, tm, lay) * 6 + which, 0, 0))


def _rms_mod(x, g, shift, scale):
    ms = jnp.mean(x * x, axis=-1, keepdims=True)
    return (x * lax.rsqrt(ms + NORM_EPS) * g) * (1.0 + scale) + shift


def _modulate_kernel(x_ref, g_ref, sh_ref, sc_ref, o_ref, *, transpose):
    h = _rms_mod(x_ref[...], g_ref[...], sh_ref[0], sc_ref[0])
    if transpose:
        o_ref[...] = h.T.astype(o_ref.dtype)
    else:
        o_ref[...] = h.astype(o_ref.dtype)


def modulate(x, g, mods, which_shift, lay, *, out_dtype=BF16, transpose=False, tm=256):
    t, d = x.shape
    tm = _tile(math.gcd(lay.ctx_seq * lay.n_ctx, lay.lat_seq), tm)
    out_shape = (d, t) if transpose else (t, d)
    out_spec = pl.BlockSpec((d, tm), lambda i: (0, i)) if transpose else pl.BlockSpec((tm, d), lambda i: (i, 0))
    return pl.pallas_call(
        functools.partial(_modulate_kernel, transpose=transpose),
        grid=(t // tm,),
        in_specs=[pl.BlockSpec((tm, d), lambda i: (i, 0)),
                  pl.BlockSpec((1, d), lambda i: (0, 0)),
                  _mod_spec(which_shift, tm, lay, d, 1),
                  _mod_spec(which_shift + 1, tm, lay, d, 1)],
        out_specs=out_spec,
        out_shape=jax.ShapeDtypeStruct(out_shape, out_dtype),
        compiler_params=_params("parallel"),
        name="modulate_t" if transpose else "modulate",
    )(x, g.reshape(1, d), mods, mods)


def _linear_kernel(*refs, has_bias, act, glu, resid, mod):
    it = iter(refs)
    a_ref = next(it)
    w_ref = next(it)
    w2_ref = next(it) if glu else None
    b_ref = next(it) if has_bias else None
    b2_ref = next(it) if (glu and has_bias) else None
    x_ref = next(it) if resid else None
    gt_ref = next(it) if resid else None
    g_ref = next(it) if mod else None
    sh_ref = next(it) if mod else None
    sc_ref = next(it) if mod else None
    o_ref = next(it)
    if mod:
        h_scr = next(it)

        @pl.when(pl.program_id(1) == 0)
        def _():
            h_scr[...] = _rms_mod(a_ref[...], g_ref[...], sh_ref[0], sc_ref[0]).astype(BF16)

        a = h_scr[...]
    else:
        a = a_ref[...].astype(BF16)
    acc = jnp.dot(a, w_ref[...], preferred_element_type=F32)
    if has_bias:
        acc = acc + b_ref[...]
    if glu:
        acc2 = jnp.dot(a, w2_ref[...], preferred_element_type=F32)
        if has_bias:
            acc2 = acc2 + b2_ref[...]
        acc = acc * jax.nn.sigmoid(acc2)
    if act == "sigmoid":
        acc = jax.nn.sigmoid(acc)
    elif act == "tanh":
        acc = jnp.tanh(acc)
    if resid:
        acc = x_ref[...] + gt_ref[0] * acc
    o_ref[...] = acc.astype(o_ref.dtype)


def linear(a, w, *, bias=None, act=None, glu=False, resid=None, mod=None, out_dtype=F32, tm=1024, tn=512,
           name="linear"):
    m, k = a.shape
    n = w.shape[1] // 2 if glu else w.shape[1]
    for opt in (resid, mod):
        if opt is not None:
            tm = _tile(math.gcd(_ctx_rows(opt[3]), opt[3].lat_seq), tm)
    tm = _tile(m, tm)
    tn = _tile(n, tn)
    nj = n // tn
    in_specs = [pl.BlockSpec((tm, k), lambda i, j: (i, 0)),
                pl.BlockSpec((k, tn), lambda i, j: (0, j))]
    args = [a, w]
    if glu:
        in_specs.append(pl.BlockSpec((k, tn), lambda i, j: (0, j + nj)))
        args.append(w)
    if bias is not None:
        b2d = bias.reshape(1, -1).astype(F32)
        in_specs.append(pl.BlockSpec((1, tn), lambda i, j: (0, j)))
        args.append(b2d)
        if glu:
            in_specs.append(pl.BlockSpec((1, tn), lambda i, j: (0, j + nj)))
            args.append(b2d)
    if resid is not None:
        x, mods, which, lay = resid
        in_specs.append(pl.BlockSpec((tm, tn), lambda i, j: (i, j)))
        args.append(x)
        in_specs.append(pl.BlockSpec((1, 1, tn), lambda i, j: (_cond_of_tile(i, tm, lay) * 6 + which, 0, j)))
        args.append(mods)
    scratch = []
    if mod is not None:
        g, mmods, which_shift, mlay = mod
        in_specs.append(pl.BlockSpec((1, k), lambda i, j: (0, 0)))
        args.append(g.reshape(1, k))
        for off in (0, 1):
            in_specs.append(_mod_spec(which_shift + off, tm, mlay, k, 2))
            args.append(mmods)
        scratch.append(pltpu.VMEM((tm, k), BF16))
    return pl.pallas_call(
        functools.partial(_linear_kernel, has_bias=bias is not None, act=act, glu=glu, resid=resid is not None,
                          mod=mod is not None),
        grid=(m // tm, nj),
        in_specs=in_specs,
        out_specs=pl.BlockSpec((tm, tn), lambda i, j: (i, j)),
        out_shape=jax.ShapeDtypeStruct((m, n), out_dtype),
        scratch_shapes=scratch,
        compiler_params=_params("parallel", "arbitrary" if mod is not None else "parallel"),
        name=name,
    )(*args)


def rope_tables(lat_seq):
    rows = lat_seq // GRID_W
    pos_row = jnp.repeat(jnp.arange(rows, dtype=F32), GRID_W)
    pos_col = jnp.tile(jnp.arange(GRID_W, dtype=F32), rows)
    n_freq = HEAD_DIM // 4
    inv = ROPE_THETA ** (-jnp.arange(n_freq, dtype=F32) / n_freq)
    ang_r = pos_row[:, None] * inv
    ang_c = pos_col[:, None] * inv
    cos = jnp.concatenate([jnp.cos(ang_r), jnp.cos(ang_r), jnp.cos(ang_c), jnp.cos(ang_c)], axis=1)
    sin = jnp.concatenate([-jnp.sin(ang_r), jnp.sin(ang_r), -jnp.sin(ang_c), jnp.sin(ang_c)], axis=1)
    return cos, sin


def _head_rms(x, g):
    ms = jnp.mean(x * x, axis=-1, keepdims=True)
    return x * lax.rsqrt(ms + NORM_EPS) * g


def _rope(x, cos, sin):
    q = HEAD_DIM // 4
    lane = lax.broadcasted_iota(jnp.int32, x.shape, 1)
    first = (lane // q) % 2 == 0
    partner = jnp.where(first, pltpu.roll(x, HEAD_DIM - q, 1), pltpu.roll(x, q, 1))
    return x * cos + partner * sin


def _qkv_post_kernel(*refs, n_heads, n_kv, rope, emit_f32):
    it = iter(refs)
    qkv_ref = next(it)
    qn_ref = next(it)
    kn_ref = next(it)
    cos_ref = next(it) if rope else None
    sin_ref = next(it) if rope else None
    q_ref = next(it)
    k_ref = next(it)
    v_ref = next(it)
    kf_ref = next(it) if emit_f32 else None
    vf_ref = next(it) if emit_f32 else None
    for h in range(n_heads + n_kv):
        x = qkv_ref[:, h * HEAD_DIM:(h + 1) * HEAD_DIM]
        is_q = h < n_heads
        y = _head_rms(x, qn_ref[...] if is_q else kn_ref[...])
        if emit_f32 and not is_q:
            kf_ref[:, (h - n_heads) * HEAD_DIM:(h - n_heads + 1) * HEAD_DIM] = y
        if rope:
            y = _rope(y, cos_ref[...], sin_ref[...])
        if is_q:
            q_ref[:, h * HEAD_DIM:(h + 1) * HEAD_DIM] = y.astype(q_ref.dtype)
        else:
            k_ref[:, (h - n_heads) * HEAD_DIM:(h - n_heads + 1) * HEAD_DIM] = y.astype(k_ref.dtype)
    v = qkv_ref[:, (n_heads + n_kv) * HEAD_DIM:]
    v_ref[...] = v.astype(v_ref.dtype)
    if emit_f32:
        vf_ref[...] = v


def qkv_post(qkv, q_norm, k_norm, *, row0, nrows, seq, rope_tabs, emit_f32, n_heads, tm=256):
    n_kv = n_heads // KV_GROUP
    dq, dk = n_heads * HEAD_DIM, n_kv * HEAD_DIM
    tm = _tile(seq, tm)
    assert row0 % tm == 0
    r0 = row0 // tm
    rope = rope_tabs is not None
    in_specs = [pl.BlockSpec((tm, dq + 2 * dk), lambda i: (i + r0, 0)),
                pl.BlockSpec((1, HEAD_DIM), lambda i: (0, 0)),
                pl.BlockSpec((1, HEAD_DIM), lambda i: (0, 0))]
    args = [qkv, q_norm.reshape(1, HEAD_DIM), k_norm.reshape(1, HEAD_DIM)]
    if rope:
        per = seq // tm
        in_specs += [pl.BlockSpec((tm, HEAD_DIM), lambda i: (i % per, 0))] * 2
        args += list(rope_tabs)
    out_shape = [jax.ShapeDtypeStruct((nrows, dq), BF16), jax.ShapeDtypeStruct((nrows, dk), BF16),
                 jax.ShapeDtypeStruct((nrows, dk), BF16)]
    out_specs = [pl.BlockSpec((tm, dq), lambda i: (i, 0)), pl.BlockSpec((tm, dk), lambda i: (i, 0)),
                 pl.BlockSpec((tm, dk), lambda i: (i, 0))]
    if emit_f32:
        out_shape += [jax.ShapeDtypeStruct((nrows, dk), F32)] * 2
        out_specs += [pl.BlockSpec((tm, dk), lambda i: (i, 0))] * 2
    return pl.pallas_call(
        functools.partial(_qkv_post_kernel, n_heads=n_heads, n_kv=n_kv, rope=rope, emit_f32=emit_f32),
        grid=(nrows // tm,),
        in_specs=in_specs, out_specs=out_specs, out_shape=out_shape,
        compiler_params=_params("parallel"),
        name="qkv_post_rope" if rope else "qkv_post",
    )(*args)


def _attn_kernel(*refs, has_cache, has_prev):
    it = iter(refs)
    q_ref = next(it)
    k_ref = next(it)
    v_ref = next(it)
    kc_ref = next(it) if has_cache else None
    vc_ref = next(it) if has_cache else None
    if has_prev:
        next(it)
    o_ref = next(it)
    c = HEAD_DIM ** -0.5 * math.log2(math.e)
    nt = (((1,), (1,)), ((), ()))
    k = k_ref[...]
    v = v_ref[...]
    for h in range(KV_GROUP):
        q = q_ref[:, h * HEAD_DIM:(h + 1) * HEAD_DIM]
        s = lax.dot_general(q, k, nt, preferred_element_type=F32)
        m = jnp.max(s, axis=-1, keepdims=True)
        if has_cache:
            s2 = lax.dot_general(q, kc_ref[0], nt, preferred_element_type=F32)
            m = jnp.maximum(m, jnp.max(s2, axis=-1, keepdims=True))
        p = jnp.exp2((s - m) * c)
        l = jnp.sum(p, axis=-1, keepdims=True)
        o = jnp.dot(p.astype(BF16), v, preferred_element_type=F32)
        if has_cache:
            p2 = jnp.exp2((s2 - m) * c)
            l = l + jnp.sum(p2, axis=-1, keepdims=True)
            o = o + jnp.dot(p2.astype(BF16), vc_ref[0], preferred_element_type=F32)
        o_ref[:, h * HEAD_DIM:(h + 1) * HEAD_DIM] = (o / l).astype(o_ref.dtype)


def attention(q, k, v, cache_k, cache_v, *, nbatch, seq, out_rows, row0, prev, tq=256):
    n_kv = k.shape[1] // HEAD_DIM
    tq = _tile(seq, tq)
    nq = seq // tq
    assert row0 % tq == 0
    r0 = row0 // tq
    gw = KV_GROUP * HEAD_DIM
    has_cache = cache_k is not None
    in_specs = [pl.BlockSpec((tq, gw), lambda b, g, i: (b * nq + i, g)),
                pl.BlockSpec((seq, HEAD_DIM), lambda b, g, i: (b, g)),
                pl.BlockSpec((seq, HEAD_DIM), lambda b, g, i: (b, g))]
    args = [q, k, v]
    if has_cache:
        past = cache_k.shape[1]
        in_specs += [pl.BlockSpec((1, past, HEAD_DIM), lambda b, g, i: (b, 0, g))] * 2
        args += [cache_k, cache_v]
    aliases = {}
    if prev is not None:
        aliases = {len(args): 0}
        in_specs.append(pl.BlockSpec(memory_space=pl.ANY))
        args.append(prev)
    return pl.pallas_call(
        functools.partial(_attn_kernel, has_cache=has_cache, has_prev=prev is not None),
        grid=(nbatch, n_kv, nq),
        in_specs=in_specs,
        out_specs=pl.BlockSpec((tq, gw), lambda b, g, i: (r0 + b * nq + i, g)),
        out_shape=jax.ShapeDtypeStruct((out_rows, q.shape[1]), BF16),
        input_output_aliases=aliases,
        compiler_params=_params("parallel", "parallel", "parallel"),
        name="attention_cache" if has_cache else "attention",
    )(*args)


def _seq_edges(i, tm, lay):
    n_ctx_tiles = _ctx_rows(lay) // tm
    per_c = lay.ctx_seq // tm
    per_l = lay.lat_seq // tm
    il = i - n_ctx_tiles
    first = jnp.where(i < n_ctx_tiles, i % per_c == 0, il % per_l == 0)
    last = jnp.where(i < n_ctx_tiles, i % per_c == per_c - 1, il % per_l == per_l - 1)
    return first, last


def _halo_specs(tm, halo, d, nblocks_total):
    r = tm // halo
    prev = pl.BlockSpec((halo, d), lambda i: (jnp.maximum(i * r - 1, 0), 0))
    nxt = pl.BlockSpec((halo, d), lambda i: (jnp.minimum((i + 1) * r, nblocks_total - 1), 0))
    return prev, nxt


CONV_HALO = 16


def _dwconv_kernel(u_ref, up_ref, un_ref, dw_ref, dwb_ref, g_ref, b_ref, o_ref, ext_ref, acc_ref, *, tm, lay, cb):
    i = pl.program_id(0)
    first, last = _seq_edges(i, tm, lay)
    d = u_ref.shape[1]
    ext_ref[CONV_HALO:CONV_HALO + tm, :] = u_ref[...]
    ext_ref[0:CONV_HALO, :] = jnp.where(first, 0.0, up_ref[...])
    ext_ref[CONV_HALO + tm:, :] = jnp.where(last, 0.0, un_ref[...])
    pad = CONV_WIDTH // 2

    def col_block(c, carry):
        cs = pl.multiple_of(c * cb, cb)
        w = ext_ref[:, pl.ds(cs, cb)]
        taps = dw_ref[:, pl.ds(cs, cb)]
        rows = w.shape[0]
        shifted = [w] + [pltpu.roll(w, rows - r, 0) for r in range(1, SUBLANES)]
        acc = jnp.zeros((tm, cb), F32)
        for j in range(CONV_WIDTH):
            q, r = divmod(CONV_HALO - pad + j, SUBLANES)
            acc = acc + shifted[r][q * SUBLANES:q * SUBLANES + tm] * taps[j:j + 1]
        acc_ref[:, pl.ds(cs, cb)] = acc
        return carry

    lax.fori_loop(0, d // cb, col_block, 0)
    y = acc_ref[...] + dwb_ref[...]
    mu = jnp.mean(y, axis=-1, keepdims=True)
    yc = y - mu
    var = jnp.mean(yc * yc, axis=-1, keepdims=True)
    z = yc * lax.rsqrt(var + LN_EPS) * g_ref[...] + b_ref[...]
    o_ref[...] = (z * jax.nn.sigmoid(z)).astype(o_ref.dtype)


def dwconv_ln_silu(u, dw, dw_b, ln_g, ln_b, lay, *, tm=128, cb=256):
    t, d = u.shape
    tm = _tile(math.gcd(lay.ctx_seq, lay.lat_seq), tm)
    cb = _tile(d, cb)
    prev, nxt = _halo_specs(tm, CONV_HALO, d, t // CONV_HALO)
    dwp = jnp.zeros((32, d), F32).at[:CONV_WIDTH].set(dw)
    row = lambda a: a.reshape(1, d)
    const = lambda r: pl.BlockSpec((r, d), lambda i: (0, 0))
    return pl.pallas_call(
        functools.partial(_dwconv_kernel, tm=tm, lay=lay, cb=cb),
        grid=(t // tm,),
        in_specs=[pl.BlockSpec((tm, d), lambda i: (i, 0)), prev, nxt,
                  const(32), const(1), const(1), const(1)],
        out_specs=pl.BlockSpec((tm, d), lambda i: (i, 0)),
        out_shape=jax.ShapeDtypeStruct((t, d), BF16),
        scratch_shapes=[pltpu.VMEM((tm + 2 * CONV_HALO, d), F32), pltpu.VMEM((tm, d), F32)],
        compiler_params=_params("parallel"),
        name="dwconv_ln_silu",
    )(u, u, u, dwp, row(dw_b), row(ln_g), row(ln_b))


MIX_HALO = 8


def _rwkv_mix_kernel(x_ref, xp_ref, xn_ref, g_ref, sh_ref, sc_ref, mu_ref, *o_refs, tm, lay):
    i = pl.program_id(0)
    first, last = _seq_edges(i, tm, lay)
    g, sh, sc = g_ref[...], sh_ref[0], sc_ref[0]
    h = _rms_mod(x_ref[...], g, sh, sc)
    hp = _rms_mod(xp_ref[MIX_HALO - 1:MIX_HALO, :], g, sh, sc)
    hn = _rms_mod(xn_ref[0:1, :], g, sh, sc)
    hp = jnp.where(first, 0.0, hp)
    hn = jnp.where(last, 0.0, hn)
    row = lax.broadcasted_iota(jnp.int32, h.shape, 0)
    h_prev = jnp.where(row == 0, hp, pltpu.roll(h, 1, 0))
    h_next = jnp.where(row == tm - 1, hn, pltpu.roll(h, tm - 1, 0))
    xx = 0.5 * (h_prev + h_next) - h
    for n, o_ref in enumerate(o_refs):
        o_ref[...] = (h + xx * mu_ref[n:n + 1, :]).astype(o_ref.dtype)


def rwkv_mix(x, g, mods, mu, lay, *, tm=256):
    t, d = x.shape
    tm = _tile(math.gcd(lay.ctx_seq, lay.lat_seq), tm)
    prev, nxt = _halo_specs(tm, MIX_HALO, d, t // MIX_HALO)
    mup = jnp.zeros((8, d), F32).at[:6].set(mu)
    return pl.pallas_call(
        functools.partial(_rwkv_mix_kernel, tm=tm, lay=lay),
        grid=(t // tm,),
        in_specs=[pl.BlockSpec((tm, d), lambda i: (i, 0)), prev, nxt,
                  pl.BlockSpec((1, d), lambda i: (0, 0)),
                  _mod_spec(0, tm, lay, d, 1), _mod_spec(1, tm, lay, d, 1),
                  pl.BlockSpec((8, d), lambda i: (0, 0))],
        out_specs=[pl.BlockSpec((tm, d), lambda i: (i, 0))] * 6,
        out_shape=[jax.ShapeDtypeStruct((t, d), BF16)] * 6,
        compiler_params=_params("parallel"),
        name="rwkv_mix",
    )(x, x, x, g.reshape(1, d), mods, mods, mup)


def _split_dot(x, w_bf16, passes, *, w_left=False):
    acc = None
    rem = x
    for _ in range(passes):
        hi = rem.astype(BF16)
        part = (jnp.dot(w_bf16, hi, preferred_element_type=F32) if w_left
                else jnp.dot(hi, w_bf16, preferred_element_type=F32))
        acc = part if acc is None else acc + part
        rem = rem - hi.astype(F32)
    return acc


def _wkv_groups(r, k, v, wl, al, kkp, ka, rk, s_old, *, reverse, masks):
    C = WKV_CHUNK
    N = RWKV_HEAD
    bd_f32, bd_bf16, tri, strict, incl, eye, off_masks = masks
    each = lambda f, *xs: [f(*a) for a in zip(*xs)]

    def bd(x):
        return jnp.concatenate([x.astype(BF16)] * WKV_GROUP, axis=0) * bd_bf16

    nt = (((1,), (1,)), ((), ()))
    dot_nt = lambda x, y: lax.dot_general(x.astype(BF16), y, nt, preferred_element_type=F32)
    dot_nn = lambda x, y: jnp.dot(x.astype(BF16), y, preferred_element_type=F32)

    iclr = each(jax.nn.sigmoid, al)
    lw = each(lambda w: -jnp.exp(-(jnp.maximum(-w, 0.0) + jnp.log1p(jnp.exp(-jnp.abs(w)))) - 0.5), wl)
    kx = each(lambda k_, p: k_ * p, k, kkp)
    kn = each(lambda x: _split_dot(x * x, bd_bf16, 2), kx)
    kk = each(lambda x, n: x / jnp.maximum(jnp.sqrt(n), 1e-12), kx, kn)
    kd = each(lambda k_, i, p: k_ * (1.0 + (i - 1.0) * p), k, iclr, ka)
    bsum = each(lambda r_, d_, p: _split_dot(r_ * d_ * p, bd_bf16, 2), r, kd, rk)
    bonus = each(lambda s, v_: s * v_, bsum, v)
    cum = each(lambda l: _split_dot(l, tri, 3, w_left=True), lw)
    g_in = each(jnp.exp, cum)
    g_inv = each(lambda c_: jnp.exp(-c_), cum)
    at = each(lambda kk_, c_, l: (-kk_ * jnp.exp(c_ - l)).astype(BF16), kk, cum, lw)
    rt = each(lambda r_, g: (r_ * g).astype(BF16), r, g_in)
    bt = each(lambda kk_, i, g: (kk_ * i * g).astype(BF16), kk, iclr, g_inv)
    kt = each(lambda d_, g: (d_ * g).astype(BF16), kd, g_inv)

    ar = each(lambda a_, r_: jnp.concatenate([a_, r_], axis=0), at, rt)
    p_b = each(lambda x, y: dot_nt(x, bd(y)), ar, bt)
    p_k = each(lambda x, y: dot_nt(x, bd(y)), ar, kt)
    p_s = each(lambda x, y: dot_nt(x, bd(y)), ar, s_old)
    bdv = each(bd, v)
    lp = each(lambda p: jnp.where(strict, p[:C], 0.0), p_b)
    rhs = each(lambda ps, pk, w: ps[:C] + dot_nn(jnp.where(strict, pk[:C], 0.0), w), p_s, p_k, bdv)
    tinv = each(lambda l: eye + jnp.where(off_masks[0], l, 0.0), lp)
    for off in off_masks[1:]:
        x = each(lambda t, l: dot_nn(t, bd(jnp.where(off, l, 0.0))), tinv, lp)
        tinv = each(lambda t, x_: t + dot_nn(x_, bd(t)), tinv, x)
    u = each(lambda t, x: dot_nn(t, bd(x)), tinv, rhs)
    y = each(lambda ps, pb, pk, u_, w: ps[C:] + dot_nn(jnp.where(incl, pb[C:], 0.0), bd(u_))
             + dot_nn(jnp.where(incl, pk[C:], 0.0), w), p_s, p_b, p_k, u, bdv)

    def new_state(u_, v_, b_, k_, s, g):
        uv = jnp.concatenate([u_, v_], axis=0)
        bk = jnp.concatenate([b_, k_], axis=0)
        f = dot_nn(uv.T, bk) * bd_f32
        zsum = f[0:N]
        for h in range(1, WKV_GROUP):
            zsum = zsum + f[h * N:(h + 1) * N]
        return (s + zsum) * (g[0:1] if reverse else g[C - 1:C])

    return y, bonus, each(new_state, u, v, bt, kt, s_old, g_in)


def _wkv_kernel(r_ref, k_ref, v_ref, wl_ref, al_ref, kkp_ref, ka_ref, rk_ref, s0_ref, *rest, reverse, nc, nsub):
    y_ref, bo_ref, sf_ref, s_scr = rest[-4:]
    C = WKV_CHUNK
    N = RWKV_HEAD
    W = WKV_GROUP * N
    c = pl.program_id(2)

    @pl.when(c == 0)
    def _():
        s_scr[...] = s0_ref[0, 0]

    lane_i = lax.broadcasted_iota(jnp.int32, (W, W), 0)
    lane_j = lax.broadcasted_iota(jnp.int32, (W, W), 1)
    bd_f32 = ((lane_i // N) == (lane_j // N)).astype(F32)
    tt = lax.broadcasted_iota(jnp.int32, (C, C), 0)
    ss = lax.broadcasted_iota(jnp.int32, (C, C), 1)
    tri = ((ss >= tt) if reverse else (ss <= tt)).astype(BF16)
    t2 = lax.broadcasted_iota(jnp.int32, (C, W), 0)
    s2 = lax.broadcasted_iota(jnp.int32, (C, W), 1) % C
    strict = (s2 > t2) if reverse else (s2 < t2)
    incl = (s2 >= t2) if reverse else (s2 <= t2)
    eye = (s2 == t2).astype(F32)
    off_masks = []
    b = 1
    while b < C:
        lo, hi = (s2, t2) if reverse else (t2, s2)
        off_masks.append(((t2 // (2 * b)) == (s2 // (2 * b))) & ((lo % (2 * b)) >= b) & ((hi % (2 * b)) < b))
        b *= 2
    masks = (bd_f32, bd_f32.astype(BF16), tri, strict, incl, eye, off_masks)

    sls = [slice(gi * W, (gi + 1) * W) for gi in range(nsub)]
    groups = lambda ref: [ref[:, sl] for sl in sls]
    y, bonus, s_new = _wkv_groups(*map(groups, (r_ref, k_ref, v_ref, wl_ref, al_ref, kkp_ref, ka_ref, rk_ref, s_scr)),
                                  reverse=reverse, masks=masks)
    for gi, sl in enumerate(sls):
        y_ref[:, sl] = y[gi]
        bo_ref[:, sl] = bonus[gi]
        s_scr[:, sl] = s_new[gi]

    @pl.when(c == nc - 1)
    def _():
        sf_ref[0, 0] = s_scr[...]


def wkv_scan(r, k, v, wl, al, k_k, k_a, r_k, s0, *, row0, nbatch, seq, reverse, nsub, prev=None):
    d = r.shape[1]
    C, W = WKV_CHUNK, WKV_GROUP * RWKV_HEAD * nsub
    assert seq % C == 0 and row0 % C == 0 and d % W == 0
    nc, ng, rb0 = seq // C, d // W, row0 // C

    def chunk(c):
        return (nc - 1 - c) if reverse else c

    tok_in = pl.BlockSpec((C, W), lambda b, g, c: (rb0 + b * nc + chunk(c), g))
    par = pl.BlockSpec((1, W), lambda b, g, c: (0, g))
    st = pl.BlockSpec((1, 1, RWKV_HEAD, W), lambda b, g, c: (b, g, 0, 0))
    row = lambda x: x.reshape(1, d)
    in_specs = [tok_in] * 5 + [par] * 3 + [st]
    args = [r, k, v, wl, al, row(k_k), row(k_a), row(r_k), s0]
    aliases = {}
    if prev is not None:
        aliases = {len(args): 0, len(args) + 1: 1}
        in_specs += [pl.BlockSpec(memory_space=pl.ANY)] * 2
        args += list(prev)
    return pl.pallas_call(
        functools.partial(_wkv_kernel, reverse=reverse, nc=nc, nsub=nsub),
        grid=(nbatch, ng, nc),
        in_specs=in_specs,
        out_specs=[tok_in, tok_in, st],
        out_shape=[jax.ShapeDtypeStruct(r.shape, F32), jax.ShapeDtypeStruct(r.shape, F32),
                   jax.ShapeDtypeStruct(s0.shape, F32)],
        input_output_aliases=aliases,
        scratch_shapes=[pltpu.VMEM((RWKV_HEAD, W), F32)],
        compiler_params=_params("parallel", "parallel", "arbitrary"),
        name="wkv_scan_rev" if reverse else "wkv_scan_fwd",
    )(*args)


def _rwkv_out_kernel(y0_ref, y1_ref, b0_ref, b1_ref, gate_ref, g_ref, b_ref, o_ref):
    d = y0_ref.shape[1]
    li = lax.broadcasted_iota(jnp.int32, (LANES, LANES), 0)
    lj = lax.broadcasted_iota(jnp.int32, (LANES, LANES), 1)
    avg = ((li // RWKV_HEAD) == (lj // RWKV_HEAD)).astype(F32) * (1.0 / RWKV_HEAD)
    for c in range(d // LANES):
        sl = slice(c * LANES, (c + 1) * LANES)
        y = y0_ref[:, sl] + y1_ref[:, sl]
        mu = jnp.dot(y, avg, precision=HIGHEST, preferred_element_type=F32)
        yc = y - mu
        var = jnp.dot(yc * yc, avg, precision=HIGHEST, preferred_element_type=F32)
        yn = yc * lax.rsqrt(var + GN_EPS) * g_ref[:, sl] + b_ref[:, sl]
        o_ref[:, sl] = ((yn + b0_ref[:, sl] + b1_ref[:, sl]) * gate_ref[:, sl]).astype(o_ref.dtype)


def rwkv_out(y0, y1, b0, b1, gate, lnx_g, lnx_b, *, tm=256):
    t, d = y0.shape
    tm = _tile(t, tm)
    tok = pl.BlockSpec((tm, d), lambda i: (i, 0))
    par = pl.BlockSpec((1, d), lambda i: (0, 0))
    return pl.pallas_call(
        _rwkv_out_kernel,
        grid=(t // tm,),
        in_specs=[tok] * 5 + [par] * 2,
        out_specs=tok,
        out_shape=jax.ShapeDtypeStruct((t, d), BF16),
        compiler_params=_params("parallel"),
        name="rwkv_out",
    )(y0, y1, b0, b1, gate, lnx_g.reshape(1, d), lnx_b.reshape(1, d))


def _sorting_network(n):
    pairs = []
    p = 1
    while p < n:
        k = p
        while k >= 1:
            for j in range(k % p, n - k, 2 * k):
                for i in range(min(k, n - j - k)):
                    if (i + j) // (2 * p) == (i + j + k) // (2 * p):
                        pairs.append((i + j, i + j + k))
            k //= 2
        p *= 2
    return pairs


def _extract_desc(x, n):
    slabs = [x[r:r + SUBLANES] for r in range(0, x.shape[0], SUBLANES)]
    for a, b in _sorting_network(len(slabs)):
        slabs[a], slabs[b] = jnp.maximum(slabs[a], slabs[b]), jnp.minimum(slabs[a], slabs[b])
    spent = jnp.full_like(slabs[0], -1.0)
    vals = []
    for t in range(n):
        mx = jnp.max(slabs[0], axis=0, keepdims=True)
        vals.append(jnp.maximum(mx, 0.0))
        won = slabs[0] == mx
        depth = min(len(slabs), n - t - 1)
        slabs = [jnp.where(won, slabs[k + 1] if k + 1 < len(slabs) else spent, slabs[k]) for k in range(depth)]
    return vals


def _peer_route_kernel(q_ref, keys_ref, ea_ref, e1_ref, th_ref):
    kp = PEER_TOPK
    half = kp // 2
    ths = []
    for h in range(PEER_HEADS):
        es = []
        tops = []
        for p in range(2):
            idx = h * 2 + p
            qhp = q_ref[idx * N_KEYS:(idx + 1) * N_KEYS, :]
            s = jnp.dot(keys_ref[idx], qhp, precision=HIGHEST, preferred_element_type=F32)
            e = jnp.exp(s - jnp.max(s, axis=0, keepdims=True))
            es.append(e)
            tops.append(jnp.concatenate(_extract_desc(e, kp), axis=0))
        v0, v1 = tops

        def cands(a0):
            blk = [a0[a:a + 1] * v1[:half] for a in range(half)]
            blk.append(a0[0:1] * v1[half:])
            blk.append(a0[half:] * v1[0:1])
            return jnp.concatenate(blk, axis=0)

        cand = cands(v0)
        best = _extract_desc(cand, kp)
        zsum = best[0]
        for b in best[1:]:
            zsum = zsum + b
        inv_z = 0.5 / zsum
        cand_n = cands(v0 * inv_z)
        th = jnp.min(jnp.where(cand >= best[kp - 1], cand_n, jnp.inf), axis=0, keepdims=True)
        ths.append(th)
        ea_ref[h * N_KEYS:(h + 1) * N_KEYS, :] = es[0] * inv_z
        e1_ref[h * N_KEYS:(h + 1) * N_KEYS, :] = es[1]
    th_ref[...] = jnp.concatenate(ths, axis=0)


def peer_route(q_t, keys, *, tn=256):
    rows, t = q_t.shape
    tn = _tile(t, tn)
    hk = PEER_HEADS * N_KEYS
    return pl.pallas_call(
        _peer_route_kernel,
        grid=(t // tn,),
        in_specs=[pl.BlockSpec((rows, tn), lambda i: (0, i)),
                  pl.BlockSpec(keys.shape, lambda i: (0, 0, 0))],
        out_specs=[pl.BlockSpec((hk, tn), lambda i: (0, i)), pl.BlockSpec((hk, tn), lambda i: (0, i)),
                   pl.BlockSpec((PEER_HEADS, tn), lambda i: (0, i))],
        out_shape=[jax.ShapeDtypeStruct((hk, t), F32), jax.ShapeDtypeStruct((hk, t), F32),
                   jax.ShapeDtypeStruct((PEER_HEADS, t), F32)],
        compiler_params=_params("parallel"),
        name="peer_route",
    )(q_t, keys)


def _peer_dense_kernel(h_ref, u_ref, vt_ref, ea_ref, e1_ref, th_ref, x_ref, gt_ref, o_ref, acc_ref, w_ref, *, ni):
    e = pl.program_id(1)

    @pl.when(e == 0)
    def _():
        acc_ref[...] = jnp.zeros_like(acc_ref)

    hv = h_ref[...]
    a = jnp.concatenate([jnp.dot(u_ref[r0:r0 + MXU_ROWS, :], hv, preferred_element_type=F32)
                         for r0 in range(0, u_ref.shape[0], MXU_ROWS)], axis=0)
    tn = h_ref.shape[1]
    lw = min(tn, 2 * LANES)
    for l0 in range(0, tn, lw):
        ls = slice(l0, l0 + lw)
        ths = [th_ref[h:h + 1, ls] for h in range(PEER_HEADS)]
        for i8 in range(ni // SUBLANES):
            base = pl.multiple_of(e * ni + i8 * SUBLANES, SUBLANES)
            tiles = [ea_ref[pl.ds(h * N_KEYS + base, SUBLANES), ls] for h in range(PEER_HEADS)]
            for ii in range(SUBLANES):
                rows = [t[ii:ii + 1] for t in tiles]
                for rb in range(N_KEYS // SUBLANES):
                    w = None
                    for h in range(PEER_HEADS):
                        r0 = h * N_KEYS + rb * SUBLANES
                        p = rows[h] * e1_ref[r0:r0 + SUBLANES, ls]
                        sel = jnp.where(p >= ths[h], p, 0.0)
                        w = sel if w is None else w + sel
                    w0 = (i8 * SUBLANES + ii) * N_KEYS + rb * SUBLANES
                    w_ref[w0:w0 + SUBLANES, ls] = w
    act = a * (1.0 + lax.erf(a * (2.0 ** -0.5)))
    gw = (w_ref[...] * act).astype(BF16)
    for r0 in range(0, acc_ref.shape[0], MXU_ROWS):
        rs = slice(r0, r0 + MXU_ROWS)
        acc_ref[rs, :] += jnp.dot(vt_ref[rs, :], gw, preferred_element_type=F32)

    @pl.when(e == pl.num_programs(1) - 1)
    def _():
        o_ref[...] = x_ref[...] + gt_ref[0] * acc_ref[...].T


def peer_dense(h_t, u_tab, vt_tab, ea, e1, th, x, mods, which_gate, lay, *, tn=512, te=1024):
    d, t = h_t.shape
    n_exp = u_tab.shape[0]
    tn = _tile(math.gcd(_ctx_rows(lay), lay.lat_seq), tn)
    te = _tile(n_exp, te)
    ni = te // N_KEYS
    assert ni % SUBLANES == 0
    hk = PEER_HEADS * N_KEYS
    route = lambda rows: pl.BlockSpec((rows, tn), lambda i, e: (0, i))
    return pl.pallas_call(
        functools.partial(_peer_dense_kernel, ni=ni),
        grid=(t // tn, n_exp // te),
        in_specs=[pl.BlockSpec((d, tn), lambda i, e: (0, i)),
                  pl.BlockSpec((te, d), lambda i, e: (e, 0)),
                  pl.BlockSpec((d, te), lambda i, e: (0, e)),
                  route(hk), route(hk), route(PEER_HEADS),
                  pl.BlockSpec((tn, d), lambda i, e: (i, 0)),
                  pl.BlockSpec((1, 1, d), lambda i, e: (_cond_of_tile(i, tn, lay) * 6 + which_gate, 0, 0))],
        out_specs=pl.BlockSpec((tn, d), lambda i, e: (i, 0)),
        out_shape=jax.ShapeDtypeStruct((t, d), F32),
        scratch_shapes=[pltpu.VMEM((d, tn), F32), pltpu.VMEM((te, tn), F32)],
        compiler_params=_params("parallel", "arbitrary"),
        name="peer_dense",
    )(h_t, u_tab, vt_tab, ea, e1, th, x, mods)


def _rmsnorm_kernel(x_ref, g_ref, o_ref):
    x = x_ref[...]
    ms = jnp.mean(x * x, axis=-1, keepdims=True)
    o_ref[...] = x * lax.rsqrt(ms + NORM_EPS) * g_ref[...]


def rmsnorm(x, g, *, row0, nrows, tm=512):
    d = x.shape[1]
    tm = _tile(math.gcd(row0, nrows) if row0 else nrows, tm)
    r0 = row0 // tm
    return pl.pallas_call(
        _rmsnorm_kernel,
        grid=(nrows // tm,),
        in_specs=[pl.BlockSpec((tm, d), lambda i: (i + r0, 0)), pl.BlockSpec((1, d), lambda i: (0, 0))],
        out_specs=pl.BlockSpec((tm, d), lambda i: (i, 0)),
        out_shape=jax.ShapeDtypeStruct((nrows, d), F32),
        compiler_params=_params("parallel"),
        name="final_rmsnorm",
    )(x, g.reshape(1, d))


def _pad_cols(w, n):
    return jnp.zeros(w.shape[:-1] + (n,), w.dtype).at[..., :w.shape[-1]].set(w)


def _pad_rows(w, n):
    return jnp.zeros((n,) + w.shape[1:], w.dtype).at[:w.shape[0]].set(w)


def _state_to_groups(s, hg):
    b, h, n, _ = s.shape
    return s.reshape(b, h // hg, hg, n, n).transpose(0, 1, 3, 2, 4).reshape(b, h // hg, n, hg * n)


def _groups_to_state(s, hg):
    b, g, n, _ = s.shape
    return s.reshape(b, g, n, hg, n).transpose(0, 1, 3, 2, 4).reshape(b, g * hg, n, n)


def attention_layer(x, mods, lay, norm_g, wqkv, q_norm, k_norm, wo, cache_k, cache_v):
    t, d = x.shape
    n_heads = d // HEAD_DIM
    n_kv = n_heads // KV_GROUP
    qkv = linear(x, wqkv.astype(BF16), mod=(norm_g, mods, 0, lay), name="attn_qkv")
    nc = _ctx_rows(lay)
    qc, kc, vc, kcf, vcf = qkv_post(qkv, q_norm, k_norm, row0=0, nrows=nc, seq=lay.ctx_seq,
                                    rope_tabs=None, emit_f32=True, n_heads=n_heads)
    ql, kl, vl = qkv_post(qkv, q_norm, k_norm, row0=nc, nrows=t - nc, seq=lay.lat_seq,
                          rope_tabs=rope_tables(lay.lat_seq), emit_f32=False, n_heads=n_heads)
    o = attention(qc, kc, vc, None, None, nbatch=lay.n_ctx, seq=lay.ctx_seq, out_rows=t, row0=0, prev=None)
    past = cache_k.shape[1]
    ck = cache_k.reshape(lay.n_lat, past, n_kv * HEAD_DIM).astype(BF16)
    cv = cache_v.reshape(lay.n_lat, past, n_kv * HEAD_DIM).astype(BF16)
    o = attention(ql, kl, vl, ck, cv, nbatch=lay.n_lat, seq=lay.lat_seq, out_rows=t, row0=nc, prev=o)
    x = linear(o, wo.astype(BF16), resid=(x, mods, 2, lay), name="attn_out")
    new_k = kcf.reshape(lay.n_ctx, lay.ctx_seq, n_kv, HEAD_DIM)
    new_v = vcf.reshape(lay.n_ctx, lay.ctx_seq, n_kv, HEAD_DIM)
    return x, new_k, new_v


def conv_layer(x, mods, lay, norm_g, w1, b1, dw, dw_b, ln_g, ln_b, w2, b2):
    u = linear(x, w1.astype(BF16), bias=b1, glu=True, mod=(norm_g, mods, 0, lay), name="conv_glu")
    z = dwconv_ln_silu(u, dw, dw_b, ln_g, ln_b, lay)
    return linear(z, w2.astype(BF16), bias=b2, resid=(x, mods, 2, lay), name="conv_out")


def rwkv_layer(x, mods, lay, norm_g, state, mu, wr, wk, wv, wo, w0, w1, w2, a0, a1, a2, g1, g2,
               k_k, k_a, r_k, lnx_g, lnx_b):
    t, d = x.shape
    xr, xw, xk, xv, xa, xg = rwkv_mix(x, norm_g, mods, mu, lay)
    bf = lambda w: w.astype(BF16)
    r = linear(xr, bf(wr), name="rwkv_r")
    k = linear(xk, bf(wk), name="rwkv_k")
    v = linear(xv, bf(wv), name="rwkv_v")
    gate = linear(linear(xg, bf(g1), act="sigmoid", out_dtype=BF16, name="rwkv_g1"), bf(g2), name="rwkv_g2")
    lora = LANES * pl.cdiv(w1.shape[-1], LANES)
    nc = _ctx_rows(lay)
    nsub = math.gcd(d // (WKV_GROUP * RWKV_HEAD), 8)
    hg = WKV_GROUP * nsub
    s_zero = jnp.zeros((lay.n_ctx, d // (hg * RWKV_HEAD), RWKV_HEAD, hg * RWKV_HEAD), F32)
    ys, bos, finals = [], [], []
    for dirn, rev in enumerate((False, True)):
        tw = linear(xw, bf(_pad_cols(w1[dirn], lora)), act="tanh", out_dtype=BF16, name="rwkv_w1")
        wl = linear(tw, bf(_pad_rows(w2[dirn], lora)), bias=w0[dirn], name="rwkv_w2")
        ta = linear(xa, bf(_pad_cols(a1[dirn], lora)), out_dtype=BF16, name="rwkv_a1")
        al = linear(ta, bf(_pad_rows(a2[dirn], lora)), bias=a0[dirn], name="rwkv_a2")
        scan = functools.partial(wkv_scan, r, k, v, wl, al, k_k, k_a, r_k.reshape(-1), reverse=rev, nsub=nsub)
        yc, bc, sf = scan(s_zero, row0=0, nbatch=lay.n_ctx, seq=lay.ctx_seq)
        y, bo, _ = scan(_state_to_groups(state[:, dirn], hg), row0=nc, nbatch=lay.n_lat, seq=lay.lat_seq,
                        prev=(yc, bc))
        ys.append(y)
        bos.append(bo)
        finals.append(_groups_to_state(sf, hg))
    o = rwkv_out(ys[0], ys[1], bos[0], bos[1], gate, lnx_g, lnx_b)
    x = linear(o, bf(wo), resid=(x, mods, 2, lay), name="rwkv_out_proj")
    return x, jnp.stack(finals, axis=1)


def peer_layer(x, mods, lay, norm_g, wq, keys, u_tab, v_tab):
    h_t = modulate(x, norm_g, mods, 3, lay, transpose=True)
    q_t = linear(wq.T.astype(BF16), h_t, name="peer_query")
    ea, e1, th = peer_route(q_t, keys.reshape(PEER_HEADS * 2, N_KEYS, -1))
    return peer_dense(h_t, u_tab.astype(BF16), v_tab.T.astype(BF16), ea, e1, th, x, mods, 5, lay)


def kernel(x_prompt, x_sample, cache_k, cache_v, state_wkv, c, c_ctx, norm1, norm2, ada_w, ada_b, attn_wqkv, attn_q_norm, attn_k_norm, attn_wo, conv_w1, conv_b1, conv_dw, conv_dw_b, conv_ln_g, conv_ln_b, conv_w2, conv_b2, rwkv_mu, rwkv_wr, rwkv_wk, rwkv_wv, rwkv_wo, rwkv_w0, rwkv_w1, rwkv_w2, rwkv_a0, rwkv_a1, rwkv_a2, rwkv_g1, rwkv_g2, rwkv_k_k, rwkv_k_a, rwkv_r_k, rwkv_lnx_g, rwkv_lnx_b, peer_wq, peer_keys, peer_u, peer_v, final_norm):
    n_ctx, ctx_seq, d = x_prompt.shape
    n_lat, lat_seq, _ = x_sample.shape
    lay = Layout(n_ctx, ctx_seq, n_lat, lat_seq)
    depth = norm1.shape[0]
    x = jnp.concatenate([x_prompt.reshape(n_ctx * ctx_seq, d), x_sample.reshape(n_lat * lat_seq, d)], axis=0)
    n_cond = 1 + n_lat
    cond = jnp.zeros((SUBLANES * pl.cdiv(n_cond, SUBLANES), d), F32).at[0].set(c_ctx).at[1:n_cond].set(c)
    cond = jax.nn.silu(cond).astype(BF16)
    new_k, new_v, new_s = [], [], []
    for i in range(depth):
        kind, j = i % 3, i // 3
        m = linear(cond, ada_w[i].astype(BF16), bias=ada_b[i], tn=1024, name="ada_mods")
        mods = m[:n_cond].reshape(n_cond * 6, 1, d)
        if kind == 0:
            x, nk, nv = attention_layer(x, mods, lay, norm1[i], attn_wqkv[j], attn_q_norm[j], attn_k_norm[j],
                                        attn_wo[j], cache_k[:, j], cache_v[:, j])
            new_k.append(nk)
            new_v.append(nv)
        elif kind == 1:
            x = conv_layer(x, mods, lay, norm1[i], conv_w1[j], conv_b1[j], conv_dw[j], conv_dw_b[j],
                           conv_ln_g[j], conv_ln_b[j], conv_w2[j], conv_b2[j])
        else:
            x, ns = rwkv_layer(x, mods, lay, norm1[i], state_wkv[:, j], rwkv_mu[j], rwkv_wr[j], rwkv_wk[j],
                               rwkv_wv[j], rwkv_wo[j], rwkv_w0[j], rwkv_w1[j], rwkv_w2[j], rwkv_a0[j],
                               rwkv_a1[j], rwkv_a2[j], rwkv_g1[j], rwkv_g2[j], rwkv_k_k[j], rwkv_k_a[j],
                               rwkv_r_k[j], rwkv_lnx_g[j], rwkv_lnx_b[j])
            new_s.append(ns)
        x = peer_layer(x, mods, lay, norm2[i], peer_wq[i], peer_keys[i], peer_u[i], peer_v[i])
    nc = n_ctx * ctx_seq
    y_ctx = rmsnorm(x, final_norm, row0=0, nrows=nc)
    y_lat = rmsnorm(x, final_norm, row0=nc, nrows=n_lat * lat_seq)
    return (y_ctx.reshape(n_ctx, ctx_seq, d), y_lat.reshape(n_lat, lat_seq, d),
            jnp.stack(new_k, axis=1), jnp.stack(new_v, axis=1), jnp.stack(new_s, axis=1))
```
